```python
import math
import jax
import jax.numpy as jnp
from jax import lax
import numpy as np

D_MODEL = 1024
BATCH = 8
SEQ = 2048
DEPTH = 4
DEC_BATCH = 128
DEC_SEQ = 4
PAST_LEN = 2048
PAGE_SIZE = 128

N_EVEN = (DEPTH + 1) // 2
N_ODD = DEPTH // 2
NORM_EPS = 1e-6

SSD_INNER = 2 * D_MODEL
SSD_HEAD_DIM = 64
SSD_HEADS = SSD_INNER // SSD_HEAD_DIM
SSD_GROUPS = 4
SSD_STATE = 128
SSD_CONV = 4
SSD_CHUNK = 64
XBC_DIM = SSD_INNER + 2 * SSD_GROUPS * SSD_STATE

MOBA_HEADS = 8
MOBA_HEAD_DIM = D_MODEL // MOBA_HEADS
MOBA_BLOCK = 256
MOBA_TOPK = 3
MOBA_QUERY_ROWS = 32
ROT_DIM = MOBA_HEAD_DIM // 4
ROPE_THETA = 500000.0

RWKV_HEAD_DIM = 64
RWKV_HEADS = D_MODEL // RWKV_HEAD_DIM
RWKV_DECAY_RANK = 64
RWKV_ICLR_RANK = 64
RWKV_GATE_RANK = 128
RWKV_GN_EPS = 64e-5
RWKV_SHIFT_DIM = 3 * D_MODEL + RWKV_DECAY_RANK + RWKV_ICLR_RANK + RWKV_GATE_RANK

GLA_HEADS = 4
GLA_KEY_DIM = D_MODEL // 2
GLA_VAL_DIM = D_MODEL
GLA_DK = GLA_KEY_DIM // GLA_HEADS
GLA_DV = GLA_VAL_DIM // GLA_HEADS
GLA_GATE_RANK = 16
GLA_GATE_NORM = 16.0
GLA_CHUNK = 64

N_MEM = 256
MEM_HEADS = 4
MEM_HEAD_DIM = D_MODEL // MEM_HEADS

D_FF = -(-(8 * D_MODEL) // (3 * 256)) * 256

EVEN_SPLITS = (SSD_INNER, SSD_INNER + XBC_DIM, SSD_INNER + XBC_DIM + SSD_HEADS,
               SSD_INNER + XBC_DIM + SSD_HEADS + D_MODEL, SSD_INNER + XBC_DIM + SSD_HEADS + 2 * D_MODEL)
EVEN_IN_DIM = SSD_INNER + XBC_DIM + SSD_HEADS + 3 * D_MODEL
RWKV_SPLITS = (D_MODEL, 2 * D_MODEL, 3 * D_MODEL, 3 * D_MODEL + RWKV_DECAY_RANK,
               3 * D_MODEL + RWKV_DECAY_RANK + RWKV_ICLR_RANK)
ODD_SPLITS = (RWKV_SHIFT_DIM, RWKV_SHIFT_DIM + GLA_KEY_DIM, RWKV_SHIFT_DIM + 2 * GLA_KEY_DIM,
              RWKV_SHIFT_DIM + 2 * GLA_KEY_DIM + GLA_VAL_DIM, RWKV_SHIFT_DIM + 2 * GLA_KEY_DIM + 2 * GLA_VAL_DIM)
ODD_IN_DIM = RWKV_SHIFT_DIM + 2 * GLA_KEY_DIM + 2 * GLA_VAL_DIM + GLA_GATE_RANK

kernel_name = 'hybrid_ssd_moba_rwkv7_gla_decode_step'


def rmsnorm(x, g):
    xf = x.astype(jnp.float32)
    y = xf * lax.rsqrt(jnp.mean(xf * xf, axis=-1, keepdims=True) + NORM_EPS)
    return (y * g.astype(jnp.float32)).astype(x.dtype)


def rope_partial(x, pos):
    half = ROT_DIM // 2
    inv_freq = ROPE_THETA ** (-jnp.arange(half, dtype=jnp.float32) / half)
    ang = pos.astype(jnp.float32)[:, None] * inv_freq[None, :]
    cos = jnp.cos(ang)[None, :, None, :]
    sin = jnp.sin(ang)[None, :, None, :]
    xr = x[..., :ROT_DIM].astype(jnp.float32)
    x1, x2 = xr[..., :half], xr[..., half:]
    rot = jnp.concatenate([x1 * cos - x2 * sin, x2 * cos + x1 * sin], axis=-1)
    return jnp.concatenate([rot.astype(x.dtype), x[..., ROT_DIM:]], axis=-1)


def causal_conv(x, prev, w, bias):
    full = jnp.concatenate([prev.astype(x.dtype), x], axis=1)
    out = lax.conv_general_dilated(full, w[:, None, :].astype(x.dtype), window_strides=(1,), padding='VALID',
                                   dimension_numbers=('NWC', 'WIO', 'NWC'), feature_group_count=x.shape[-1])
    return out + bias, full[:, full.shape[1] - (SSD_CONV - 1):]


def _to_chunks(t, q):
    b, L = t.shape[0], t.shape[1]
    return t.astype(jnp.float32).reshape(b, L // q, q, *t.shape[2:]).swapaxes(0, 1)


def ssd_scan(xh, dt, A, Bm, Cm, S0):
    b, L, H, P = xh.shape
    G, N = Bm.shape[2], Bm.shape[3]
    R = H // G
    Q = SSD_CHUNK if L % SSD_CHUNK == 0 else L
    xs = _to_chunks(xh.reshape(b, L, G, R, P), Q)
    dts = _to_chunks(dt.reshape(b, L, G, R), Q)
    Bs = _to_chunks(Bm, Q)
    Cs = _to_chunks(Cm, Q)
    A_gr = A.astype(jnp.float32).reshape(G, R)
    causal = jnp.tril(jnp.ones((Q, Q), dtype=bool))

    def step(S, inp):
        xc, dtc, Bc, Cc = inp
        cs = jnp.cumsum(dtc * A_gr, axis=1)
        seg = cs[:, :, None] - cs[:, None, :]
        decay = jnp.exp(jnp.where(causal[None, :, :, None, None], seg, -jnp.inf))
        cb = jnp.einsum('bign,bjgn->bijg', Cc, Bc)
        w = cb[..., None] * decay * dtc[:, None]
        y = jnp.einsum('bijgr,bjgrp->bigrp', w, xc)
        y = y + jnp.einsum('bign,bgrpn->bigrp', Cc, S) * jnp.exp(cs)[..., None]
        tail = jnp.exp(cs[:, -1:] - cs) * dtc
        S = S * jnp.exp(cs[:, -1])[..., None, None] + jnp.einsum('bjgr,bjgrp,bjgn->bgrpn', tail, xc, Bc)
        return S, y

    S, ys = lax.scan(step, S0.astype(jnp.float32).reshape(b, G, R, P, N), (xs, dts, Bs, Cs))
    return ys.swapaxes(0, 1).reshape(b, L, H, P), S.reshape(b, H, P, N).astype(S0.dtype)


def gla_scan(q, k, v, logf, S0):
    b, L = q.shape[0], q.shape[1]
    Q = GLA_CHUNK if L % GLA_CHUNK == 0 else L
    causal = jnp.tril(jnp.ones((Q, Q), dtype=bool))

    def step(S, inp):
        qc, kc, vc, fc = inp
        cb = jnp.cumsum(fc, axis=1)
        rel = cb[:, :, None] - cb[:, None, :]
        rel = jnp.exp(jnp.where(causal[None, :, :, None, None], rel, -jnp.inf))
        att = jnp.einsum('bihd,bjhd,bijhd->bijh', qc, kc, rel)
        o = jnp.einsum('bijh,bjhv->bihv', att, vc) + jnp.einsum('bihd,bhdv->bihv', qc * jnp.exp(cb), S)
        S = S * jnp.exp(cb[:, -1])[..., None] + jnp.einsum('bjhd,bjhv->bhdv', kc * jnp.exp(cb[:, -1:] - cb), vc)
        return S, o

    xs = (_to_chunks(q, Q), _to_chunks(k, Q), _to_chunks(v, Q), _to_chunks(logf, Q))
    S, os_ = lax.scan(step, S0.astype(jnp.float32), xs)
    return os_.swapaxes(0, 1).reshape(b, L, *os_.shape[3:]), S.astype(S0.dtype)


def rwkv7_scan(r, w, k, v, kk, a, S0):
    def step(S, inp):
        rt, wt, kt, vt, kkt, at = inp
        sa = jnp.einsum('bhvk,bhk->bhv', S, -kkt)
        S = S * wt[:, :, None, :] + sa[..., None] * (kkt * at)[:, :, None, :] + vt[..., None] * kt[:, :, None, :]
        return S, jnp.einsum('bhvk,bhk->bhv', S, rt)

    xs = (r.swapaxes(0, 1), w.swapaxes(0, 1), k.swapaxes(0, 1), v.swapaxes(0, 1), kk.swapaxes(0, 1), a.swapaxes(0, 1))
    S, ys = lax.scan(step, S0.astype(jnp.float32), xs)
    return ys.swapaxes(0, 1), S.astype(S0.dtype)


def _query_chunk(b, lq):
    target = min(max(1, MOBA_QUERY_ROWS // b), 128)
    qc = 1
    while qc * 2 <= target and lq % (qc * 2) == 0:
        qc *= 2
    return qc


def moba_attention(q, k_blk, v_blk, q0):
    b, lq, H, d = q.shape
    nb = k_blk.shape[1]
    n_sel = min(MOBA_TOPK, (q0 + lq - 1) // MOBA_BLOCK)
    qc = _query_chunk(b, lq)
    scale = d ** -0.5
    kmean = jnp.mean(k_blk.astype(jnp.float32), axis=2)
    gather = jax.vmap(jax.vmap(lambda blocks, idx: blocks[idx], in_axes=(2, 0)))

    def attend(i):
        start = i * qc
        qs = lax.dynamic_slice_in_dim(q, start, qc, axis=1)
        pos = q0 + start + jnp.arange(qc)
        bt = (q0 + start) // MOBA_BLOCK
        k_own = lax.dynamic_index_in_dim(k_blk, bt, axis=1, keepdims=False)
        v_own = lax.dynamic_index_in_dim(v_blk, bt, axis=1, keepdims=False)
        kpos = bt * MOBA_BLOCK + jnp.arange(MOBA_BLOCK)
        s_own = jnp.einsum('bqhd,bkhd->bhqk', qs, k_own).astype(jnp.float32) * scale
        s_own = jnp.where(kpos[None, :] <= pos[:, None], s_own, -jnp.inf)
        if n_sel == 0:
            p_own = jax.nn.softmax(s_own, axis=-1).astype(v_blk.dtype)
            return jnp.einsum('bhqk,bkhd->bqhd', p_own, v_own)
        gate = jnp.einsum('bqhd,bnhd->bhqn', qs.astype(jnp.float32), kmean)
        gate = jnp.where(jnp.arange(nb) < bt, gate, -jnp.inf)
        _, idx = lax.top_k(gate, n_sel)
        ok = idx < bt
        k_sel = gather(k_blk, idx)
        v_sel = gather(v_blk, idx)
        s_sel = jnp.einsum('bqhd,bhqskd->bhqsk', qs, k_sel).astype(jnp.float32) * scale
        s_sel = jnp.where(ok[..., None], s_sel, -jnp.inf).reshape(b, H, qc, n_sel * MOBA_BLOCK)
        p = jax.nn.softmax(jnp.concatenate([s_sel, s_own], axis=-1), axis=-1).astype(v_blk.dtype)
        p_sel = p[..., :n_sel * MOBA_BLOCK].reshape(b, H, qc, n_sel, MOBA_BLOCK)
        return (jnp.einsum('bhqsk,bhqskd->bqhd', p_sel, v_sel)
                + jnp.einsum('bhqk,bkhd->bqhd', p[..., n_sel * MOBA_BLOCK:], v_own))

    out = lax.map(attend, jnp.arange(lq // qc))
    return out.swapaxes(0, 1).reshape(b, lq, H, d)


def even_mixer(h, P, e, S0, conv_prev, k_past, v_past):
    b, L, _ = h.shape
    f32 = jnp.float32
    q0 = k_past.shape[1]
    proj = h @ P['e_w_in'][e]
    z, xbc, dt, q, k, v = jnp.split(proj, EVEN_SPLITS, axis=-1)
    xbc, conv_new = causal_conv(xbc, conv_prev, P['e_conv_w'][e], P['e_conv_b'][e])
    xbc = jax.nn.silu(xbc)
    xs, Bm, Cm = jnp.split(xbc, (SSD_INNER, SSD_INNER + SSD_GROUPS * SSD_STATE), axis=-1)
    dt = jax.nn.softplus(dt.astype(f32) + P['e_dt_bias'][e].astype(f32))
    A = -jnp.exp(P['e_a_log'][e].astype(f32))
    xh = xs.reshape(b, L, SSD_HEADS, SSD_HEAD_DIM)
    y, S = ssd_scan(xh, dt, A, Bm.reshape(b, L, SSD_GROUPS, SSD_STATE), Cm.reshape(b, L, SSD_GROUPS, SSD_STATE), S0)
    y = (y + P['e_d_skip'][e].astype(f32)[:, None] * xh.astype(f32)).reshape(b, L, SSD_INNER)
    y = y * jax.nn.silu(z.astype(f32))
    y = rmsnorm(y.reshape(b, L, SSD_GROUPS, SSD_INNER // SSD_GROUPS),
                P['e_g_ssd'][e].reshape(SSD_GROUPS, SSD_INNER // SSD_GROUPS)).reshape(b, L, SSD_INNER)
    pos = q0 + jnp.arange(L)
    q = rope_partial(rmsnorm(q.reshape(b, L, MOBA_HEADS, MOBA_HEAD_DIM), P['e_g_q'][e]), pos)
    k = rope_partial(rmsnorm(k.reshape(b, L, MOBA_HEADS, MOBA_HEAD_DIM), P['e_g_k'][e]), pos)
    v = v.reshape(b, L, MOBA_HEADS, MOBA_HEAD_DIM)
    pad = (-(q0 + L)) % MOBA_BLOCK
    zpad = jnp.zeros((b, pad, MOBA_HEADS, MOBA_HEAD_DIM), k.dtype)
    k_blk = jnp.concatenate([k_past.astype(k.dtype), k, zpad], axis=1).reshape(b, -1, MOBA_BLOCK, MOBA_HEADS, MOBA_HEAD_DIM)
    v_blk = jnp.concatenate([v_past.astype(v.dtype), v, zpad], axis=1).reshape(b, -1, MOBA_BLOCK, MOBA_HEADS, MOBA_HEAD_DIM)
    o = moba_attention(q, k_blk, v_blk, q0).reshape(b, L, D_MODEL)
    out = jnp.concatenate([y.astype(h.dtype), o.astype(h.dtype)], axis=-1) @ P['e_w_out'][e]
    return out, S, conv_new, k, v


def odd_mixer(h, P, o, S_r0, shift_prev, S_g0):
    b, L, _ = h.shape
    f32 = jnp.float32
    proj = h @ P['o_w_in'][o]
    cur, gq, gk, gv, gg, gf = jnp.split(proj, ODD_SPLITS, axis=-1)
    prev = jnp.concatenate([shift_prev[:, None].astype(cur.dtype), cur[:, :-1]], axis=1)
    mixed = cur + (prev - cur) * P['o_mu'][o]
    r, kc, vc, wd, ad, gd = jnp.split(mixed, RWKV_SPLITS, axis=-1)
    logw = -jax.nn.softplus(-(P['o_w0'][o] + jnp.tanh(wd) @ P['o_w_w2'][o]).astype(f32)) - 0.5
    decay = jnp.exp(-jnp.exp(logw))
    a = jax.nn.sigmoid((P['o_a0'][o] + ad @ P['o_w_a2'][o]).astype(f32))
    g = (jax.nn.sigmoid(gd) @ P['o_w_g2'][o]).astype(f32)
    hs = (b, L, RWKV_HEADS, RWKV_HEAD_DIM)
    r = r.astype(f32).reshape(hs)
    kc = kc.astype(f32).reshape(hs)
    vc = vc.astype(f32).reshape(hs)
    decay = decay.reshape(hs)
    a = a.reshape(hs)
    kk = kc * P['o_k_k'][o].astype(f32).reshape(RWKV_HEADS, RWKV_HEAD_DIM)
    kk = kk * lax.rsqrt(jnp.maximum(jnp.sum(kk * kk, axis=-1, keepdims=True), 1e-24))
    kc = kc * (1.0 + (a - 1.0) * P['o_k_a'][o].astype(f32).reshape(RWKV_HEADS, RWKV_HEAD_DIM))
    y, S_r = rwkv7_scan(r, decay, kc, vc, kk, a, S_r0)
    mu = jnp.mean(y, axis=-1, keepdims=True)
    var = jnp.mean(jnp.square(y - mu), axis=-1, keepdims=True)
    y = ((y - mu) * lax.rsqrt(var + RWKV_GN_EPS)).reshape(b, L, D_MODEL) * P['o_gn_w'][o] + P['o_gn_b'][o]
    bonus = jnp.sum(r * kc * P['o_r_k'][o].astype(f32), axis=-1, keepdims=True) * vc
    y = (y + bonus.reshape(b, L, D_MODEL)) * g
    q = gq.astype(f32).reshape(b, L, GLA_HEADS, GLA_DK) * GLA_DK ** -0.5
    k = gk.astype(f32).reshape(b, L, GLA_HEADS, GLA_DK)
    v = gv.astype(f32).reshape(b, L, GLA_HEADS, GLA_DV)
    logf = jax.nn.log_sigmoid((gf @ P['o_w_f2'][o] + P['o_b_f'][o]).astype(f32)) / GLA_GATE_NORM
    og, S_g = gla_scan(q, k, v, logf.reshape(b, L, GLA_HEADS, GLA_DK), S_g0)
    og = rmsnorm(og, P['o_g_gla'][o]).reshape(b, L, GLA_VAL_DIM) * jax.nn.silu(gg.astype(f32))
    out = jnp.concatenate([y, og], axis=-1).astype(h.dtype) @ P['o_w_out'][o]
    return out, S_r, cur[:, -1], S_g


def memory_kv(mem, P):
    b = mem.shape[0]
    ks, vs = [], []
    for l in range(DEPTH):
        kv = rmsnorm(mem, P['norm_memtok'][l]) @ P['m_w_kv'][l]
        k, v = jnp.split(kv, 2, axis=-1)
        ks.append(rmsnorm(k.reshape(b, N_MEM, MEM_HEADS, MEM_HEAD_DIM), P['m_g_k'][l]))
        vs.append(v.reshape(b, N_MEM, MEM_HEADS, MEM_HEAD_DIM))
    return jnp.stack(ks), jnp.stack(vs)


def mem_attention(h, mk, mv, w_q, g_q, w_o):
    b, L, _ = h.shape
    q = rmsnorm((h @ w_q).reshape(b, L, MEM_HEADS, MEM_HEAD_DIM), g_q)
    s = jnp.einsum('blhd,bmhd->bhlm', q, mk.astype(q.dtype)).astype(jnp.float32) * MEM_HEAD_DIM ** -0.5
    p = jax.nn.softmax(s, axis=-1).astype(h.dtype)
    o = jnp.einsum('bhlm,bmhd->blhd', p, mv.astype(h.dtype))
    return o.reshape(b, L, D_MODEL) @ w_o


def swiglu(h, w_gu, w_down):
    g, u = jnp.split(h @ w_gu, 2, axis=-1)
    return (jax.nn.silu(g) * u) @ w_down


def paged_rows(cache, e, page_table):
    rows = cache[e, page_table]
    return rows.reshape(page_table.shape[0], page_table.shape[1] * PAGE_SIZE, MOBA_HEADS, MOBA_HEAD_DIM)


def trunk(x, mem_k, mem_v, ssd0, conv0, past_rows, rwkv0, shift0, gla0, P):
    ssd_l, conv_l, k_l, v_l, rwkv_l, shift_l, gla_l = [], [], [], [], [], [], []
    for l in range(DEPTH):
        h = rmsnorm(x, P['norm_mix'][l])
        if l % 2 == 0:
            e = l // 2
            k_past, v_past = past_rows(e)
            y, S, c, k, v = even_mixer(h, P, e, ssd0[e], conv0[e], k_past, v_past)
            ssd_l.append(S)
            conv_l.append(c)
            k_l.append(k)
            v_l.append(v)
        else:
            o = l // 2
            y, S_r, sh, S_g = odd_mixer(h, P, o, rwkv0[o], shift0[o], gla0[o])
            rwkv_l.append(S_r)
            shift_l.append(sh)
            gla_l.append(S_g)
        x = x + y.astype(x.dtype)
        x = x + mem_attention(rmsnorm(x, P['norm_mem'][l]), mem_k[l], mem_v[l],
                              P['m_w_q'][l], P['m_g_q'][l], P['m_w_o'][l]).astype(x.dtype)
        x = x + swiglu(rmsnorm(x, P['norm_ffn'][l]), P['f_w_gu'][l], P['f_w_down'][l]).astype(x.dtype)
    return x, jnp.stack(k_l), jnp.stack(v_l), jnp.stack(ssd_l), jnp.stack(conv_l), jnp.stack(rwkv_l), jnp.stack(shift_l), jnp.stack(gla_l)


def setup_inputs(seed: int = 0) -> dict:
    key = jax.random.key(seed)
    ks = iter(jax.random.split(key, 64))
    f32 = jnp.float32

    def nrm(shape, scale=1.0):
        return jax.random.normal(next(ks), shape, f32) * scale

    def gain(shape):
        return 1.0 + 0.02 * nrm(shape)

    n_pages = PAST_LEN // PAGE_SIZE
    n_used = DEC_BATCH * n_pages
    n_pool = n_used + n_used // 4
    x_prompt = nrm((BATCH, SEQ, D_MODEL))
    x_sample = nrm((DEC_BATCH, DEC_SEQ, D_MODEL))
    cache_moba_k = nrm((N_EVEN, n_pool, PAGE_SIZE, MOBA_HEADS, MOBA_HEAD_DIM))
    cache_moba_v = nrm((N_EVEN, n_pool, PAGE_SIZE, MOBA_HEADS, MOBA_HEAD_DIM))
    state_ssd = nrm((N_EVEN, DEC_BATCH, SSD_HEADS, SSD_HEAD_DIM, SSD_STATE), 0.1)
    state_ssd_conv = nrm((N_EVEN, DEC_BATCH, SSD_CONV - 1, XBC_DIM))
    state_rwkv = nrm((N_ODD, DEC_BATCH, RWKV_HEADS, RWKV_HEAD_DIM, RWKV_HEAD_DIM), 0.1)
    state_rwkv_shift = nrm((N_ODD, DEC_BATCH, RWKV_SHIFT_DIM))
    state_gla = nrm((N_ODD, DEC_BATCH, GLA_HEADS, GLA_DK, GLA_DV), 0.1)
    cache_mem_k = nrm((DEPTH, DEC_BATCH, N_MEM, MEM_HEADS, MEM_HEAD_DIM))
    cache_mem_v = nrm((DEPTH, DEC_BATCH, N_MEM, MEM_HEADS, MEM_HEAD_DIM))
    perm = jax.random.permutation(next(ks), n_pool)
    page_table = perm[:n_used].reshape(DEC_BATCH, n_pages).astype(jnp.int32)
    mem_prompt = nrm((BATCH, N_MEM, D_MODEL))
    dt0 = jnp.exp(jax.random.uniform(next(ks), (N_EVEN, SSD_HEADS), f32, math.log(1e-3), math.log(1e-1)))
    return {
        'x_prompt': x_prompt, 'x_sample': x_sample,
        'cache_moba_k': cache_moba_k, 'cache_moba_v': cache_moba_v,
        'state_ssd': state_ssd, 'state_ssd_conv': state_ssd_conv,
        'state_rwkv': state_rwkv, 'state_rwkv_shift': state_rwkv_shift, 'state_gla': state_gla,
        'cache_mem_k': cache_mem_k, 'cache_mem_v': cache_mem_v,
        'page_table': page_table, 'mem_prompt': mem_prompt,
        'norm_mix': gain((DEPTH, D_MODEL)), 'norm_mem': gain((DEPTH, D_MODEL)),
        'norm_memtok': gain((DEPTH, D_MODEL)), 'norm_ffn': gain((DEPTH, D_MODEL)),
        'e_w_in': nrm((N_EVEN, D_MODEL, EVEN_IN_DIM), D_MODEL ** -0.5),
        'e_conv_w': nrm((N_EVEN, SSD_CONV, XBC_DIM), SSD_CONV ** -0.5),
        'e_conv_b': nrm((N_EVEN, XBC_DIM), 0.01),
        'e_dt_bias': dt0 + jnp.log(-jnp.expm1(-dt0)),
        'e_a_log': jnp.log(jax.random.uniform(next(ks), (N_EVEN, SSD_HEADS), f32, 1.0, 16.0)),
        'e_d_skip': 1.0 + 0.1 * nrm((N_EVEN, SSD_HEADS)),
        'e_g_ssd': gain((N_EVEN, SSD_INNER)),
        'e_g_q': gain((N_EVEN, MOBA_HEAD_DIM)), 'e_g_k': gain((N_EVEN, MOBA_HEAD_DIM)),
        'e_w_out': nrm((N_EVEN, SSD_INNER + D_MODEL, D_MODEL), (SSD_INNER + D_MODEL) ** -0.5),
        'o_w_in': nrm((N_ODD, D_MODEL, ODD_IN_DIM), D_MODEL ** -0.5),
        'o_mu': jax.random.uniform(next(ks), (N_ODD, RWKV_SHIFT_DIM), f32),
        'o_w0': -2.0 + 0.5 * nrm((N_ODD, D_MODEL)),
        'o_w_w2': nrm((N_ODD, RWKV_DECAY_RANK, D_MODEL), 0.5 * RWKV_DECAY_RANK ** -0.5),
        'o_a0': 0.5 * nrm((N_ODD, D_MODEL)),
        'o_w_a2': nrm((N_ODD, RWKV_ICLR_RANK, D_MODEL), RWKV_ICLR_RANK ** -0.5),
        'o_w_g2': nrm((N_ODD, RWKV_GATE_RANK, D_MODEL), RWKV_GATE_RANK ** -0.5),
        'o_k_k': 0.85 + 0.05 * nrm((N_ODD, D_MODEL)),
        'o_k_a': 1.0 + 0.05 * nrm((N_ODD, D_MODEL)),
        'o_r_k': 0.1 * nrm((N_ODD, RWKV_HEADS, RWKV_HEAD_DIM)),
        'o_gn_w': gain((N_ODD, D_MODEL)), 'o_gn_b': 0.01 * nrm((N_ODD, D_MODEL)),
        'o_w_f2': nrm((N_ODD, GLA_GATE_RANK, GLA_KEY_DIM), GLA_GATE_RANK ** -0.5),
        'o_b_f': 1.0 + 0.1 * nrm((N_ODD, GLA_KEY_DIM)),
        'o_g_gla': gain((N_ODD, GLA_DV)),
        'o_w_out': nrm((N_ODD, 2 * D_MODEL, D_MODEL), (2 * D_MODEL) ** -0.5),
        'm_w_q': nrm((DEPTH, D_MODEL, D_MODEL), D_MODEL ** -0.5),
        'm_w_kv': nrm((DEPTH, D_MODEL, 2 * D_MODEL), D_MODEL ** -0.5),
        'm_g_q': gain((DEPTH, MEM_HEAD_DIM)), 'm_g_k': gain((DEPTH, MEM_HEAD_DIM)),
        'm_w_o': nrm((DEPTH, D_MODEL, D_MODEL), D_MODEL ** -0.5),
        'f_w_gu': nrm((DEPTH, D_MODEL, 2 * D_FF), D_MODEL ** -0.5),
        'f_w_down': nrm((DEPTH, D_FF, D_MODEL), D_FF ** -0.5),
    }


def reference(x_prompt, x_sample, cache_moba_k, cache_moba_v, state_ssd, state_ssd_conv, state_rwkv,
              state_rwkv_shift, state_gla, cache_mem_k, cache_mem_v, page_table, mem_prompt,
              norm_mix, norm_mem, norm_memtok, norm_ffn,
              e_w_in, e_conv_w, e_conv_b, e_dt_bias, e_a_log, e_d_skip, e_g_ssd, e_g_q, e_g_k, e_w_out,
              o_w_in, o_mu, o_w0, o_w_w2, o_a0, o_w_a2, o_w_g2, o_k_k, o_k_a, o_r_k, o_gn_w, o_gn_b,
              o_w_f2, o_b_f, o_g_gla, o_w_out,
              m_w_q, m_w_kv, m_g_q, m_g_k, m_w_o, f_w_gu, f_w_down):
    P = dict(norm_mix=norm_mix, norm_mem=norm_mem, norm_memtok=norm_memtok, norm_ffn=norm_ffn,
             e_w_in=e_w_in, e_conv_w=e_conv_w, e_conv_b=e_conv_b, e_dt_bias=e_dt_bias, e_a_log=e_a_log,
             e_d_skip=e_d_skip, e_g_ssd=e_g_ssd, e_g_q=e_g_q, e_g_k=e_g_k, e_w_out=e_w_out,
             o_w_in=o_w_in, o_mu=o_mu, o_w0=o_w0, o_w_w2=o_w_w2, o_a0=o_a0, o_w_a2=o_w_a2, o_w_g2=o_w_g2,
             o_k_k=o_k_k, o_k_a=o_k_a, o_r_k=o_r_k, o_gn_w=o_gn_w, o_gn_b=o_gn_b, o_w_f2=o_w_f2, o_b_f=o_b_f,
             o_g_gla=o_g_gla, o_w_out=o_w_out, m_w_q=m_w_q, m_w_kv=m_w_kv, m_g_q=m_g_q, m_g_k=m_g_k,
             m_w_o=m_w_o, f_w_gu=f_w_gu, f_w_down=f_w_down)
    bp = x_prompt.shape[0]
    fdt = x_prompt.dtype
    mem_k_p, mem_v_p = memory_kv(mem_prompt, P)
    empty = jnp.zeros((bp, 0, MOBA_HEADS, MOBA_HEAD_DIM), fdt)
    y_p, k_p, v_p, ssd_p, conv_p, rwkv_p, shift_p, gla_p = trunk(
        x_prompt, mem_k_p, mem_v_p,
        jnp.zeros((N_EVEN, bp, SSD_HEADS, SSD_HEAD_DIM, SSD_STATE), fdt),
        jnp.zeros((N_EVEN, bp, SSD_CONV - 1, XBC_DIM), fdt),
        lambda e: (empty, empty),
        jnp.zeros((N_ODD, bp, RWKV_HEADS, RWKV_HEAD_DIM, RWKV_HEAD_DIM), fdt),
        jnp.zeros((N_ODD, bp, RWKV_SHIFT_DIM), fdt),
        jnp.zeros((N_ODD, bp, GLA_HEADS, GLA_DK, GLA_DV), fdt), P)
    y_s, k_s, v_s, ssd_s, conv_s, rwkv_s, shift_s, gla_s = trunk(
        x_sample, cache_mem_k, cache_mem_v, state_ssd, state_ssd_conv,
        lambda e: (paged_rows(cache_moba_k, e, page_table), paged_rows(cache_moba_v, e, page_table)),
        state_rwkv, state_rwkv_shift, state_gla, P)
    return (y_p, y_s, k_p, v_p, k_s, v_s, ssd_p, ssd_s, conv_p, conv_s, rwkv_p, rwkv_s,
            shift_p, shift_s, gla_p, gla_s, mem_k_p, mem_v_p)
```

```python
import functools
import math

import jax
import jax.numpy as jnp
from jax import lax
from jax.experimental import pallas as pl
from jax.experimental.pallas import tpu as pltpu

F32 = jnp.float32
BF16 = jnp.bfloat16

D_MODEL = 1024
DEPTH = 4
N_EVEN = 2
N_ODD = 2
NORM_EPS = 1e-6
PAGE_SIZE = 128

SSD_INNER = 2048
SSD_HEAD_DIM = 64
SSD_HEADS = 32
SSD_GROUPS = 4
SSD_STATE = 128
SSD_CONV = 4
XBC_DIM = SSD_INNER + 2 * SSD_GROUPS * SSD_STATE
SSD_Q = 128

MOBA_HEADS = 8
MOBA_HEAD_DIM = 128
MOBA_BLOCK = 256
MOBA_TOPK = 3
ROT_DIM = 32
ROPE_THETA = 500000.0

RWKV_HEAD_DIM = 64
RWKV_HEADS = 16
RWKV_GN_EPS = 64e-5
RWKV_SHIFT_DIM = 3 * D_MODEL + 64 + 64 + 128
RWKV_Q = 64

GLA_HEADS = 4
GLA_DK = 128
GLA_DV = 256
GLA_KEY_DIM = 512
GLA_GATE_RANK = 16
GLA_GATE_NORM = 16.0
GLA_Q = 64

N_MEM = 256
MEM_HEADS = 4
MEM_HEAD_DIM = 256
D_FF = 2816

LANES = 128
VMEM_LIMIT = 56 * 1024 * 1024


def _cp(sem, vmem=VMEM_LIMIT):
    return pltpu.CompilerParams(dimension_semantics=sem, vmem_limit_bytes=vmem)


def _bdot(a, b):
    return jnp.dot(a.astype(BF16), b.astype(BF16), preferred_element_type=F32)


def _bdot_nt(a, b):
    return lax.dot_general(a.astype(BF16), b.astype(BF16), (((1,), (1,)), ((), ())),
                           preferred_element_type=F32)


def _bdot_tn(a, b):
    k = a.shape[0]
    kp = -(-k // LANES) * LANES
    if kp != k:
        a = jnp.concatenate([a, jnp.zeros((kp - k, a.shape[1]), a.dtype)], axis=0)
        b = jnp.concatenate([b, jnp.zeros((kp - k, b.shape[1]), b.dtype)], axis=0)
    return jnp.dot(a.T.astype(BF16), b.astype(BF16), preferred_element_type=F32)


def _split3(x):
    hi = x.astype(BF16)
    r1 = x - hi.astype(F32)
    mid = r1.astype(BF16)
    lo = (r1 - mid.astype(F32)).astype(BF16)
    return hi, mid, lo


def _exact_left(m01, x):
    hi, mid, lo = _split3(x)
    m = m01.astype(BF16)
    return (jnp.dot(m, hi, preferred_element_type=F32) + jnp.dot(m, mid, preferred_element_type=F32)
            + jnp.dot(m, lo, preferred_element_type=F32))


def _exact_right(x, m01):
    hi, mid, lo = _split3(x)
    m = m01.astype(BF16)
    return (jnp.dot(hi, m, preferred_element_type=F32) + jnp.dot(mid, m, preferred_element_type=F32)
            + jnp.dot(lo, m, preferred_element_type=F32))


def _f32_nt(a, b):
    ah, am, al = _split3(a)
    bh, bm, bl = _split3(b)
    dn = (((1,), (1,)), ((), ()))
    d = lambda x, y: lax.dot_general(x, y, dn, preferred_element_type=F32)
    return d(ah, bh) + (d(ah, bm) + d(am, bh)) + (d(am, bm) + d(ah, bl) + d(al, bh))


def _rms(x, g, eps=NORM_EPS):
    return x * lax.rsqrt(jnp.mean(x * x, axis=-1, keepdims=True) + eps) * g


def _sigmoid(x):
    return 1.0 / (1.0 + jnp.exp(-x))


def _silu(x):
    return x * _sigmoid(x)


def _softplus(x):
    return jnp.maximum(x, 0.0) + jnp.log1p(jnp.exp(-jnp.abs(x)))


def _iota(shape, dim):
    return lax.broadcasted_iota(jnp.int32, shape, dim)


def _div(x, d):
    return lax.shift_right_logical(x, jnp.int32(int(math.log2(d))))


def _mod(x, d):
    return x & jnp.int32(d - 1)


def _pad_rows(x, rows):
    if x.shape[0] == rows:
        return x
    return jnp.concatenate([x, jnp.zeros((rows - x.shape[0], x.shape[1]), x.dtype)], axis=0)


def _seg_ones(seg):
    r = _div(_iota((LANES, LANES), 0), seg)
    c = _div(_iota((LANES, LANES), 1), seg)
    return (r == c).astype(BF16)


def _seg_sum(x, ones_bd):
    parts = []
    for s in range(x.shape[1] // LANES):
        parts.append(_exact_right(x[:, s * LANES:(s + 1) * LANES], ones_bd))
    return parts[0] if len(parts) == 1 else jnp.concatenate(parts, axis=1)


def _mm_kernel(*refs, n_a, has_gain, hn_dim, has_res):
    pos = 0
    a_refs = refs[pos:pos + n_a]
    pos += n_a
    g_ref = None
    if has_gain:
        g_ref = refs[pos]
        pos += 1
    w_refs = refs[pos:pos + n_a]
    pos += n_a
    hn_ref = None
    if hn_dim:
        hn_ref = refs[pos]
        pos += 1
    res_ref = None
    if has_res:
        res_ref = refs[pos]
        pos += 1
    o_ref = refs[pos]
    pos += 1
    if has_gain:
        an_ref = refs[pos]

        @pl.when(pl.program_id(1) == 0)
        def _():
            an_ref[...] = _rms(a_refs[0][...].astype(F32), g_ref[...]).astype(BF16)

        acc = jnp.dot(an_ref[...], w_refs[0][...], preferred_element_type=F32)
    else:
        acc = None
        for a_ref, w_ref in zip(a_refs, w_refs):
            d = jnp.dot(a_ref[...].astype(BF16), w_ref[...], preferred_element_type=F32)
            acc = d if acc is None else acc + d
    if hn_dim:
        parts = [_rms(acc[:, s * hn_dim:(s + 1) * hn_dim], hn_ref[...]) for s in range(acc.shape[1] // hn_dim)]
        acc = parts[0] if len(parts) == 1 else jnp.concatenate(parts, axis=1)
    if has_res:
        acc = acc + res_ref[...]
    o_ref[...] = acc.astype(o_ref.dtype)


def _mm(a_list, w_list, *, gain=None, head_gain=None, res=None, tm=1024, tn=1024, out_dtype=F32, name="mm"):
    m = a_list[0].shape[0]
    n = w_list[0].shape[1]
    tm = min(tm, m)
    tn = min(tn, n)
    assert m % tm == 0 and n % tn == 0, (m, tm, n, tn)
    has_gain = gain is not None
    assert not has_gain or len(a_list) == 1
    in_specs = [pl.BlockSpec((tm, a.shape[1]), lambda i, j: (i, 0)) for a in a_list]
    args = list(a_list)
    if has_gain:
        in_specs.append(pl.BlockSpec((1, a_list[0].shape[1]), lambda i, j: (0, 0)))
        args.append(gain.reshape(1, -1))
    for w in w_list:
        in_specs.append(pl.BlockSpec((w.shape[0], tn), lambda i, j: (0, j)))
        args.append(w)
    hn_dim = 0
    if head_gain is not None:
        hn_dim = head_gain.shape[-1]
        in_specs.append(pl.BlockSpec((1, hn_dim), lambda i, j: (0, 0)))
        args.append(head_gain.reshape(1, -1))
    if res is not None:
        in_specs.append(pl.BlockSpec((tm, tn), lambda i, j: (i, j)))
        args.append(res)
    scratch = [pltpu.VMEM((tm, a_list[0].shape[1]), BF16)] if has_gain else []
    return pl.pallas_call(
        functools.partial(_mm_kernel, n_a=len(a_list), has_gain=has_gain, hn_dim=hn_dim, has_res=res is not None),
        grid=(m // tm, n // tn),
        in_specs=in_specs,
        out_specs=pl.BlockSpec((tm, tn), lambda i, j: (i, j)),
        out_shape=jax.ShapeDtypeStruct((m, n), out_dtype),
        scratch_shapes=scratch,
        compiler_params=_cp(("parallel", "arbitrary")),
        name=name,
    )(*args)


def _ffn_kernel(x_ref, g_ref, wg_ref, wu_ref, wd_ref, o_ref, an_ref, acc_ref):
    f = pl.program_id(1)

    @pl.when(f == 0)
    def _():
        an_ref[...] = _rms(x_ref[...], g_ref[...]).astype(BF16)
        acc_ref[...] = jnp.zeros_like(acc_ref)

    a = an_ref[...]
    g = jnp.dot(a, wg_ref[...], preferred_element_type=F32)
    u = jnp.dot(a, wu_ref[...], preferred_element_type=F32)
    act = (_silu(g) * u).astype(BF16)
    acc_ref[...] += jnp.dot(act, wd_ref[...], preferred_element_type=F32)

    @pl.when(f == pl.num_programs(1) - 1)
    def _():
        o_ref[...] = x_ref[...] + acc_ref[...]


def _ffn(x, gain, wg, wu, wd, *, tm=1024, tf=256):
    m = x.shape[0]
    tm = min(tm, m)
    nf = D_FF // tf
    return pl.pallas_call(
        _ffn_kernel,
        grid=(m // tm, nf),
        in_specs=[pl.BlockSpec((tm, D_MODEL), lambda i, f: (i, 0)),
                  pl.BlockSpec((1, D_MODEL), lambda i, f: (0, 0)),
                  pl.BlockSpec((D_MODEL, tf), lambda i, f: (0, f)),
                  pl.BlockSpec((D_MODEL, tf), lambda i, f: (0, f)),
                  pl.BlockSpec((tf, D_MODEL), lambda i, f: (f, 0))],
        out_specs=pl.BlockSpec((tm, D_MODEL), lambda i, f: (i, 0)),
        out_shape=jax.ShapeDtypeStruct((m, D_MODEL), F32),
        scratch_shapes=[pltpu.VMEM((tm, D_MODEL), BF16), pltpu.VMEM((tm, D_MODEL), F32)],
        compiler_params=_cp(("parallel", "arbitrary")),
        name="ffn",
    )(x, gain.reshape(1, -1), wg, wu, wd)


def _qkprep_kernel(q_ref, k_ref, cos_ref, sin_ref, gq_ref, gk_ref, qo_ref, ko_ref, *maybe_km, with_kmean):
    cos = cos_ref[...]
    sin = sin_ref[...]
    lane = _iota(cos.shape, 1)
    half = ROT_DIM // 2

    def prep(x_ref, g_ref, o_ref):
        for h in range(MOBA_HEADS):
            sl = slice(h * MOBA_HEAD_DIM, (h + 1) * MOBA_HEAD_DIM)
            y = _rms(x_ref[:, sl], g_ref[...])
            sw = jnp.where(lane < half, pltpu.roll(y, MOBA_HEAD_DIM - half, 1), pltpu.roll(y, half, 1))
            o_ref[:, sl] = y * cos + sw * sin

    prep(q_ref, gq_ref, qo_ref)
    prep(k_ref, gk_ref, ko_ref)
    if with_kmean:
        maybe_km[0][0] = jnp.mean(ko_ref[...], axis=0, keepdims=True)


def _rope_tables(pos):
    half = ROT_DIM // 2
    inv_freq = ROPE_THETA ** (-jnp.arange(half, dtype=F32) / half)
    ang = pos.astype(F32)[:, None] * inv_freq[None, :]
    cos, sin = jnp.cos(ang), jnp.sin(ang)
    ones = jnp.ones((pos.shape[0], MOBA_HEAD_DIM - ROT_DIM), F32)
    cos_t = jnp.concatenate([cos, cos, ones], axis=1)
    sin_t = jnp.concatenate([-sin, sin, 0.0 * ones], axis=1)
    return cos_t, sin_t


def _qk_prep(proj, q_col, k_col, cos_t, sin_t, gq, gk, *, tq, with_kmean):
    t = proj.shape[0]
    n_tab = cos_t.shape[0] // tq
    out_shape = [jax.ShapeDtypeStruct((t, D_MODEL), F32), jax.ShapeDtypeStruct((t, D_MODEL), F32)]
    out_specs = [pl.BlockSpec((tq, D_MODEL), lambda i: (i, 0)), pl.BlockSpec((tq, D_MODEL), lambda i: (i, 0))]
    if with_kmean:
        out_shape.append(jax.ShapeDtypeStruct((t // tq, 1, D_MODEL), F32))
        out_specs.append(pl.BlockSpec((1, 1, D_MODEL), lambda i: (i, 0, 0)))
    return pl.pallas_call(
        functools.partial(_qkprep_kernel, with_kmean=with_kmean),
        grid=(t // tq,),
        in_specs=[pl.BlockSpec((tq, D_MODEL), lambda i: (i, q_col)),
                  pl.BlockSpec((tq, D_MODEL), lambda i: (i, k_col)),
                  pl.BlockSpec((tq, MOBA_HEAD_DIM), lambda i: (i % n_tab, 0)),
                  pl.BlockSpec((tq, MOBA_HEAD_DIM), lambda i: (i % n_tab, 0)),
                  pl.BlockSpec((1, MOBA_HEAD_DIM), lambda i: (0, 0)),
                  pl.BlockSpec((1, MOBA_HEAD_DIM), lambda i: (0, 0))],
        out_specs=out_specs,
        out_shape=out_shape,
        compiler_params=_cp(("parallel",)),
        name="qk_prep",
    )(proj, proj, cos_t, sin_t, gq.reshape(1, -1), gk.reshape(1, -1))


def _moba_prompt_kernel(q_ref, k_ref, v_ref, km_ref, o_ref, *, nb, n_sel):
    qi = pl.program_id(2)
    blk = MOBA_BLOCK
    scale = MOBA_HEAD_DIM ** -0.5
    q = q_ref[...]
    qb = q.astype(BF16)
    km = _pad_rows(km_ref[0], LANES)
    gates = _f32_nt(q, km)
    col = _iota((blk, LANES), 1)
    past = col < qi
    ri = _iota((blk, blk), 0)
    ci = _iota((blk, blk), 1)

    own = pl.ds(pl.multiple_of(qi * blk, blk), blk)
    s = _bdot_nt(qb, k_ref[own, :]) * scale
    s = jnp.where(ci <= ri, s, -jnp.inf)
    m = jnp.max(s, axis=-1, keepdims=True)
    p = jnp.exp(s - m)
    l = jnp.sum(p, axis=-1, keepdims=True)
    acc = _bdot(p, v_ref[own, :])

    def body(n, carry):
        m, l, acc = carry
        g_n = jnp.sum(jnp.where(col == n, gates, 0.0), axis=-1, keepdims=True)
        beats = past & ((gates > g_n) | ((gates == g_n) & (col < n)))
        rank = jnp.sum(beats.astype(F32), axis=-1, keepdims=True)
        sel = rank < n_sel
        rows = pl.ds(pl.multiple_of(n * blk, blk), blk)
        s = _bdot_nt(qb, k_ref[rows, :]) * scale
        s = jnp.where(sel, s, -jnp.inf)
        m_new = jnp.maximum(m, jnp.max(s, axis=-1, keepdims=True))
        alpha = jnp.exp(m - m_new)
        p = jnp.exp(s - m_new)
        l = alpha * l + jnp.sum(p, axis=-1, keepdims=True)
        acc = alpha * acc + _bdot(p, v_ref[rows, :])
        return m_new, l, acc

    m, l, acc = lax.fori_loop(0, qi, body, (m, l, acc))
    o_ref[...] = acc / l


def _moba_prompt(q, k, v, kmean, *, b, l):
    nb = l // MOBA_BLOCK
    n_sel = min(MOBA_TOPK, (l - 1) // MOBA_BLOCK)
    hd = MOBA_HEAD_DIM
    return pl.pallas_call(
        functools.partial(_moba_prompt_kernel, nb=nb, n_sel=n_sel),
        grid=(b, MOBA_HEADS, nb),
        in_specs=[pl.BlockSpec((MOBA_BLOCK, hd), lambda bi, h, qi: (bi * nb + qi, h)),
                  pl.BlockSpec((l, hd), lambda bi, h, qi: (bi, h)),
                  pl.BlockSpec((l, hd), lambda bi, h, qi: (bi, h)),
                  pl.BlockSpec((1, nb, hd), lambda bi, h, qi: (bi, 0, h))],
        out_specs=pl.BlockSpec((MOBA_BLOCK, hd), lambda bi, h, qi: (bi * nb + qi, h)),
        out_shape=jax.ShapeDtypeStruct((b * l, D_MODEL), F32),
        compiler_params=_cp(("parallel", "parallel", "arbitrary")),
        name="moba_prompt",
    )(q, k, v, kmean)


def _moba_sample_kernel(pt_ref, q_ref, kn_ref, vn_ref, kc_ref, vc_ref, o_ref,
                        qbd_s, m_s, l_s, acc_s, ks_s, *, n_pages, l_real, n_sel):
    del pt_ref
    p = pl.program_id(1)
    hd = MOBA_HEAD_DIM
    nh = MOBA_HEADS
    rows = l_real * nh
    scale = hd ** -0.5
    pages_per_block = MOBA_BLOCK // PAGE_SIZE
    head_of_lane = _div(_iota((nh, D_MODEL), 1), hd)
    sub = _iota((nh, D_MODEL), 0)
    diag = head_of_lane == sub

    @pl.when(p == 0)
    def _():
        q = q_ref[0]
        for t in range(l_real):
            qt = jnp.broadcast_to(q[t:t + 1, :], (nh, D_MODEL))
            qbd_s[t * nh:(t + 1) * nh, :] = jnp.where(diag, qt, 0.0)

    kp = kc_ref[0]
    vp = vc_ref[0]
    qbd = qbd_s[...]
    s = _bdot_nt(qbd, kp) * scale
    m = jnp.max(s, axis=-1, keepdims=True)
    pr = jnp.exp(s - m)
    m_s[p] = m
    l_s[p] = jnp.sum(pr, axis=-1, keepdims=True)
    acc_s[p] = _bdot(pr, vp)
    ksum = jnp.sum(kp, axis=0, keepdims=True)
    nblk = p // pages_per_block

    @pl.when(p % pages_per_block == 0)
    def _():
        ks_s[pl.ds(nblk, 1), :] = ksum

    @pl.when(p % pages_per_block != 0)
    def _():
        ks_s[pl.ds(nblk, 1), :] = ks_s[pl.ds(nblk, 1), :] + ksum

    @pl.when(p == n_pages - 1)
    def _():
        n_blocks = n_pages // pages_per_block
        kmean = _pad_rows(ks_s[...] * (1.0 / MOBA_BLOCK), LANES)
        gates = _f32_nt(qbd, kmean)
        col = _iota((rows, LANES), 1)
        past = col < n_blocks
        sels = []
        for n in range(n_blocks):
            g_n = gates[:, n:n + 1]
            beats = past & ((gates > g_n) | ((gates == g_n) & (col < n)))
            rank = jnp.sum(beats.astype(F32), axis=-1, keepdims=True)
            sels.append(rank < n_sel)
        kn = _pad_rows(kn_ref[0], LANES)
        vn = _pad_rows(vn_ref[0], LANES)
        s_own = _bdot_nt(qbd, kn) * scale
        tok = _div(_iota((rows, LANES), 0), nh)
        s_own = jnp.where(col <= tok, s_own, -jnp.inf)
        m_own = jnp.max(s_own, axis=-1, keepdims=True)
        big = m_own
        for pg in range(n_pages):
            big = jnp.maximum(big, jnp.where(sels[pg // pages_per_block], m_s[pg], -jnp.inf))
        p_own = jnp.exp(s_own - big)
        den = jnp.sum(p_own, axis=-1, keepdims=True)
        num = _bdot(p_own, vn)
        for pg in range(n_pages):
            w = jnp.where(sels[pg // pages_per_block], jnp.exp(m_s[pg] - big), 0.0)
            den = den + w * l_s[pg]
            num = num + w * acc_s[pg]
        out = num / den
        o_ref[0] = jnp.zeros(o_ref.shape[1:], F32)
        for t in range(l_real):
            blk = jnp.where(diag, out[t * nh:(t + 1) * nh, :], 0.0)
            o_ref[0, t:t + 1, :] = jnp.sum(blk, axis=0, keepdims=True)


def _moba_sample(page_table, q, k_new, v_new, cache_k, cache_v, *, layer_off, l_real):
    b, n_pages = page_table.shape
    lp = q.shape[1]
    q0 = n_pages * PAGE_SIZE
    assert q0 % MOBA_BLOCK == 0 and l_real <= lp <= MOBA_BLOCK
    n_sel = min(MOBA_TOPK, (q0 + l_real - 1) // MOBA_BLOCK)
    rows = l_real * MOBA_HEADS
    tok_spec = pl.BlockSpec((1, lp, D_MODEL), lambda bi, p, pt: (bi, 0, 0))
    page_spec = pl.BlockSpec((1, PAGE_SIZE, D_MODEL), lambda bi, p, pt: (layer_off + pt[bi, p], 0, 0))
    grid_spec = pltpu.PrefetchScalarGridSpec(
        num_scalar_prefetch=1,
        grid=(b, n_pages),
        in_specs=[tok_spec, tok_spec, tok_spec, page_spec, page_spec],
        out_specs=tok_spec,
        scratch_shapes=[pltpu.VMEM((rows, D_MODEL), F32),
                        pltpu.VMEM((n_pages, rows, 1), F32),
                        pltpu.VMEM((n_pages, rows, 1), F32),
                        pltpu.VMEM((n_pages, rows, D_MODEL), F32),
                        pltpu.VMEM((n_pages * PAGE_SIZE // MOBA_BLOCK, D_MODEL), F32)],
    )
    return pl.pallas_call(
        functools.partial(_moba_sample_kernel, n_pages=n_pages, l_real=l_real, n_sel=n_sel),
        grid_spec=grid_spec,
        out_shape=jax.ShapeDtypeStruct((b, lp, D_MODEL), F32),
        compiler_params=_cp(("parallel", "arbitrary")),
        name="moba_sample",
    )(page_table, q, k_new, v_new, cache_k, cache_v)


def _ssd_kernel(*refs, lb, has_state):
    if has_state:
        (z_ref, x_ref, b_ref, c_ref, dt_ref, dtt_ref, arow_ref, acol_ref, cwx_ref, cwb_ref, cwc_ref,
         cbx_ref, cbb_ref, cbc_ref, dl_ref, gs_ref, px_ref, pb_ref, pc_ref, s0_ref,
         y_ref, so_ref, s_s, tx_s, tb_s, tc_s) = refs
    else:
        (z_ref, x_ref, b_ref, c_ref, dt_ref, dtt_ref, arow_ref, acol_ref, cwx_ref, cwb_ref, cwc_ref,
         cbx_ref, cbb_ref, cbc_ref, dl_ref, gs_ref,
         y_ref, so_ref, s_s, tx_s, tb_s, tc_s) = refs
    c = pl.program_id(2)
    q = SSD_Q
    kc = SSD_CONV

    @pl.when(c == 0)
    def _():
        if has_state:
            s_s[...] = s0_ref[0]
            for t_s, p_ref in ((tx_s, px_ref), (tb_s, pb_ref), (tc_s, pc_ref)):
                t_s[...] = jnp.zeros_like(t_s)
                t_s[8 - (kc - 1):8, :] = p_ref[0]
        else:
            s_s[...] = jnp.zeros_like(s_s)
            tx_s[...] = jnp.zeros_like(tx_s)
            tb_s[...] = jnp.zeros_like(tb_s)
            tc_s[...] = jnp.zeros_like(tc_s)

    def conv(x, tail_s, w_ref, bias_ref):
        prev8 = tail_s[...]
        row8 = _iota((8, x.shape[1]), 0)
        out = x * w_ref[kc - 1:kc, :]
        for j in range(1, kc):
            xs = pltpu.roll(x, j, 0)
            head = jnp.where(row8 < j, pltpu.roll(prev8, j, 0), xs[0:8])
            sh = head if lb == 8 else jnp.concatenate([head, xs[8:]], axis=0)
            out = out + sh * w_ref[kc - 1 - j:kc - j, :]
        tail_s[...] = x[lb - 8:lb]
        return _silu(out + bias_ref[...])

    xc = _pad_rows(conv(x_ref[0], tx_s, cwx_ref, cbx_ref), q)
    bc = _pad_rows(conv(b_ref[0], tb_s, cwb_ref, cbb_ref), q)
    cc = _pad_rows(conv(c_ref[0], tc_s, cwc_ref, cbc_ref), q)
    dt = dt_ref[0, 0]
    dtt = dtt_ref[0, 0]
    a_row = -jnp.exp(arow_ref[0])
    a_col = -jnp.exp(acol_ref[0])
    ri = _iota((q, q), 0)
    ci = _iota((q, q), 1)
    causal = ri >= ci
    cs = _exact_left(causal, dt * a_row)
    cst = _exact_right(dtt * a_col, ri <= ci)
    cb = _bdot_nt(cc, bc)
    lane = _iota((q, LANES), 1)
    low = lane < SSD_HEAD_DIM
    row128 = _iota((2 * SSD_HEAD_DIM, 1), 0)
    ys = []
    for pp in range(4):
        ha, hb = 2 * pp, 2 * pp + 1
        xp = xc[:, pp * LANES:(pp + 1) * LANES]
        w_heads = []
        for h in (ha, hb):
            seg = cs[:, h:h + 1] - cst[h:h + 1, :]
            dec = jnp.where(causal, jnp.exp(seg), 0.0)
            w_heads.append(cb * dec * dtt[h:h + 1, :])
        y = _bdot(w_heads[0], jnp.where(low, xp, 0.0)) + _bdot(w_heads[1], jnp.where(low, 0.0, xp))
        sp = s_s[pp]
        csa, csb = cs[:, ha:ha + 1], cs[:, hb:hb + 1]
        y = y + _bdot_nt(cc, sp) * jnp.where(low, jnp.exp(csa), jnp.exp(csb))
        y = y + dl_ref[:, pp * LANES:(pp + 1) * LANES] * xp
        la, lbb = csa[q - 1:q, :], csb[q - 1:q, :]
        tail = jnp.where(low, jnp.exp(la - csa) * dt[:, ha:ha + 1], jnp.exp(lbb - csb) * dt[:, hb:hb + 1])
        upd = _bdot_tn(xp * tail, bc)
        s_s[pp] = sp * jnp.where(row128 < SSD_HEAD_DIM, jnp.exp(la), jnp.exp(lbb)) + upd
        ys.append(y)
    y = jnp.concatenate(ys, axis=1)[0:lb]
    y = y * _silu(z_ref[0])
    y_ref[0] = _rms(y, gs_ref[...])
    so_ref[0] = s_s[...]


def _ssd(pm, dt, dtt, conv_w, conv_b, a_log, d_skip, g_ssd, conv_prev, s0, *, lb):
    b, l, _ = pm.shape
    nc = l // lb
    hg = SSD_HEADS // SSD_GROUPS
    gw = SSD_INNER // SSD_GROUPS
    has_state = s0 is not None
    cwx, cwb, cwc = conv_w[:, :SSD_INNER], conv_w[:, SSD_INNER:SSD_INNER + 512], conv_w[:, SSD_INNER + 512:]
    cbx, cbb, cbc = (conv_b[None, :SSD_INNER], conv_b[None, SSD_INNER:SSD_INNER + 512],
                     conv_b[None, SSD_INNER + 512:])
    a_pad = jnp.pad(a_log.reshape(SSD_GROUPS, 1, hg), ((0, 0), (0, 0), (0, LANES - hg)))
    a_col = a_log.reshape(SSD_GROUPS, hg, 1)
    d_lane = jnp.repeat(d_skip, SSD_HEAD_DIM)[None, :]
    in_specs = [
        pl.BlockSpec((1, lb, gw), lambda bi, g, c: (bi, c, g)),
        pl.BlockSpec((1, lb, gw), lambda bi, g, c: (bi, c, 4 + g)),
        pl.BlockSpec((1, lb, LANES), lambda bi, g, c: (bi, c, 32 + g)),
        pl.BlockSpec((1, lb, LANES), lambda bi, g, c: (bi, c, 36 + g)),
        pl.BlockSpec((1, 1, SSD_Q, LANES), lambda bi, g, c: (bi, g, c, 0)),
        pl.BlockSpec((1, 1, hg, SSD_Q), lambda bi, g, c: (bi, g, 0, c)),
        pl.BlockSpec((1, 1, LANES), lambda bi, g, c: (g, 0, 0)),
        pl.BlockSpec((1, hg, 1), lambda bi, g, c: (g, 0, 0)),
        pl.BlockSpec((SSD_CONV, gw), lambda bi, g, c: (0, g)),
        pl.BlockSpec((SSD_CONV, LANES), lambda bi, g, c: (0, g)),
        pl.BlockSpec((SSD_CONV, LANES), lambda bi, g, c: (0, g)),
        pl.BlockSpec((1, gw), lambda bi, g, c: (0, g)),
        pl.BlockSpec((1, LANES), lambda bi, g, c: (0, g)),
        pl.BlockSpec((1, LANES), lambda bi, g, c: (0, g)),
        pl.BlockSpec((1, gw), lambda bi, g, c: (0, g)),
        pl.BlockSpec((1, gw), lambda bi, g, c: (0, g)),
    ]
    args = [pm, pm, pm, pm, dt, dtt, a_pad, a_col, cwx, cwb, cwc, cbx, cbb, cbc, d_lane, g_ssd[None, :]]
    if has_state:
        in_specs += [
            pl.BlockSpec((1, SSD_CONV - 1, gw), lambda bi, g, c: (bi, 0, g)),
            pl.BlockSpec((1, SSD_CONV - 1, LANES), lambda bi, g, c: (bi, 0, 16 + g)),
            pl.BlockSpec((1, SSD_CONV - 1, LANES), lambda bi, g, c: (bi, 0, 20 + g)),
            pl.BlockSpec((1, 4, LANES, LANES), lambda bi, g, c: (bi, g, 0, 0)),
        ]
        args += [conv_prev, conv_prev, conv_prev, s0]
    y, s_out = pl.pallas_call(
        functools.partial(_ssd_kernel, lb=lb, has_state=has_state),
        grid=(b, SSD_GROUPS, nc),
        in_specs=in_specs,
        out_specs=[pl.BlockSpec((1, lb, gw), lambda bi, g, c: (bi, c, g)),
                   pl.BlockSpec((1, 4, LANES, LANES), lambda bi, g, c: (bi, g, 0, 0))],
        out_shape=[jax.ShapeDtypeStruct((b, l, SSD_INNER), F32),
                   jax.ShapeDtypeStruct((b, SSD_HEADS // 2, LANES, LANES), F32)],
        scratch_shapes=[pltpu.VMEM((4, LANES, LANES), F32), pltpu.VMEM((8, gw), F32),
                        pltpu.VMEM((8, LANES), F32), pltpu.VMEM((8, LANES), F32)],
        compiler_params=_cp(("parallel", "parallel", "arbitrary")),
        name="ssd",
    )(*args)
    return y, s_out


def _dt_kernel(x_ref, g_ref, w_ref, wt_ref, bias_ref, biast_ref, o_ref, ot_ref):
    a = _rms(x_ref[...], g_ref[...]).astype(BF16)
    o_ref[...] = _softplus(jnp.dot(a, w_ref[...], preferred_element_type=F32) + bias_ref[...])
    t = lax.dot_general(wt_ref[...], a, (((1,), (1,)), ((), ())), preferred_element_type=F32)
    ot_ref[...] = _softplus(t + biast_ref[...])


def _dt_proj(x, gain, w_dt, dt_bias, *, tm=1024):
    m = x.shape[0]
    tm = min(tm, m)
    w_pad = jnp.pad(w_dt, ((0, 0), (0, LANES - SSD_HEADS))).astype(BF16)
    wt = w_dt.T.astype(BF16)
    bias_pad = jnp.pad(dt_bias, (0, LANES - SSD_HEADS))[None, :]
    return pl.pallas_call(
        _dt_kernel,
        grid=(m // tm,),
        in_specs=[pl.BlockSpec((tm, D_MODEL), lambda i: (i, 0)),
                  pl.BlockSpec((1, D_MODEL), lambda i: (0, 0)),
                  pl.BlockSpec((D_MODEL, LANES), lambda i: (0, 0)),
                  pl.BlockSpec((SSD_HEADS, D_MODEL), lambda i: (0, 0)),
                  pl.BlockSpec((1, LANES), lambda i: (0, 0)),
                  pl.BlockSpec((SSD_HEADS, 1), lambda i: (0, 0))],
        out_specs=[pl.BlockSpec((tm, LANES), lambda i: (i, 0)),
                   pl.BlockSpec((SSD_HEADS, tm), lambda i: (0, i))],
        out_shape=[jax.ShapeDtypeStruct((m, LANES), F32), jax.ShapeDtypeStruct((SSD_HEADS, m), F32)],
        compiler_params=_cp(("parallel",)),
        name="dt_proj",
    )(x, gain.reshape(1, -1), w_pad, wt, bias_pad, dt_bias[:, None])


def _gla_levels(n_valid):
    s, levels = 1, []
    while s < n_valid:
        levels.append(s)
        s *= 2
    return levels[::-1]


def _gla_kernel(*refs, lb, n_valid, has_state):
    if has_state:
        (v_ref, gg_ref, q_ref, k_ref, gf_ref, wf_ref, bf_ref, gn_ref, s0_ref, o_ref, so_ref, st_s) = refs
    else:
        (v_ref, gg_ref, q_ref, k_ref, gf_ref, wf_ref, bf_ref, gn_ref, o_ref, so_ref, st_s) = refs
    c = pl.program_id(2)
    qn = GLA_Q

    @pl.when(c == 0)
    def _():
        if has_state:
            st_s[...] = s0_ref[0, 0].T
        else:
            st_s[...] = jnp.zeros_like(st_s)

    q = _pad_rows(q_ref[0], qn) * (GLA_DK ** -0.5)
    k = _pad_rows(k_ref[0], qn)
    v = _pad_rows(v_ref[0], qn)
    logits = _bdot(_pad_rows(gf_ref[0], qn), wf_ref[...]) + bf_ref[...]
    logf = -_softplus(-logits) * (1.0 / GLA_GATE_NORM)
    rowl = _iota((qn, LANES), 0)
    logf = jnp.where(rowl < n_valid, logf, 0.0)

    ri = _iota((qn, qn), 0)
    ci = _iota((qn, qn), 1)
    levels = _gla_levels(min(n_valid, qn))
    mats = [ri >= ci]
    for s in levels:
        same = _div(ri, s) == _div(ci, s)
        mats.append(same & (ri >= ci))
        mats.append(same)
    stack = jnp.concatenate([mm.astype(BF16) for mm in mats], axis=0)
    cums = _exact_left(stack, logf)
    cb = cums[0:qn]

    att = jnp.where(ri == ci, jnp.sum(q * k, axis=-1, keepdims=True), 0.0)
    for idx, s in enumerate(levels):
        lq = cums[(1 + 2 * idx) * qn:(2 + 2 * idx) * qn]
        bs = cums[(2 + 2 * idx) * qn:(3 + 2 * idx) * qn]
        a = _bdot_nt(q * jnp.exp(lq), k * jnp.exp(bs - lq))
        mask = ((_div(ri, 2 * s) == _div(ci, 2 * s)) & (_mod(_div(ri, s), 2) == 1)
                & (_mod(_div(ci, s), 2) == 0))
        att = att + jnp.where(mask, a, 0.0)

    st = st_s[...]
    o = _bdot(att, v) + _bdot_nt(q * jnp.exp(cb), st)
    cbl = cb[qn - 1:qn, :]
    st_s[...] = st * jnp.exp(cbl) + _bdot_tn(v, k * jnp.exp(cbl - cb))
    og = _rms(o[0:lb], gn_ref[...]) * _silu(gg_ref[0])
    o_ref[0] = og

    @pl.when(c == pl.num_programs(2) - 1)
    def _():
        so_ref[0, 0] = st_s[...].T


def _gla(pg, gf, w_f2, b_f, g_gla, s0, *, lb, n_valid):
    b, l, _ = pg.shape
    nc = l // lb
    has_state = s0 is not None
    wf_pad = jnp.pad(w_f2, ((0, LANES - GLA_GATE_RANK), (0, 0))).astype(BF16)
    in_specs = [
        pl.BlockSpec((1, lb, GLA_DV), lambda bi, h, c: (bi, c, h)),
        pl.BlockSpec((1, lb, GLA_DV), lambda bi, h, c: (bi, c, 4 + h)),
        pl.BlockSpec((1, lb, GLA_DK), lambda bi, h, c: (bi, c, 16 + h)),
        pl.BlockSpec((1, lb, GLA_DK), lambda bi, h, c: (bi, c, 20 + h)),
        pl.BlockSpec((1, lb, LANES), lambda bi, h, c: (bi, c, 0)),
        pl.BlockSpec((LANES, GLA_DK), lambda bi, h, c: (0, h)),
        pl.BlockSpec((1, GLA_DK), lambda bi, h, c: (0, h)),
        pl.BlockSpec((1, GLA_DV), lambda bi, h, c: (0, 0)),
    ]
    args = [pg, pg, pg, pg, gf, wf_pad, b_f[None, :], g_gla[None, :]]
    if has_state:
        in_specs.append(pl.BlockSpec((1, 1, GLA_DK, GLA_DV), lambda bi, h, c: (bi, h, 0, 0)))
        args.append(s0)
    return pl.pallas_call(
        functools.partial(_gla_kernel, lb=lb, n_valid=n_valid, has_state=has_state),
        grid=(b, GLA_HEADS, nc),
        in_specs=in_specs,
        out_specs=[pl.BlockSpec((1, lb, GLA_DV), lambda bi, h, c: (bi, c, h)),
                   pl.BlockSpec((1, 1, GLA_DK, GLA_DV), lambda bi, h, c: (bi, h, 0, 0))],
        out_shape=[jax.ShapeDtypeStruct((b, l, GLA_HEADS * GLA_DV), F32),
                   jax.ShapeDtypeStruct((b, GLA_HEADS, GLA_DK, GLA_DV), F32)],
        scratch_shapes=[pltpu.VMEM((GLA_DV, GLA_DK), F32)],
        compiler_params=_cp(("parallel", "parallel", "arbitrary")),
        name="gla",
    )(*args)


def _rwkv_prep_kernel(*refs, tiles_per_seq, prev_given):
    if prev_given:
        (cur_ref, prev_ref, mu_ref, w12_ref, wg2_ref, w0_ref, a0_ref, kk_ref, ka_ref,
         r_o, lw_o, k_o, v_o, kk_o, a_o, g_o) = refs
    else:
        (cur_ref, mu_ref, w12_ref, wg2_ref, w0_ref, a0_ref, kk_ref, ka_ref,
         r_o, lw_o, k_o, v_o, kk_o, a_o, g_o, last_s) = refs
    d = D_MODEL
    cur = cur_ref[...]
    if prev_given:
        prev = prev_ref[...]
    else:
        i = pl.program_id(0)

        @pl.when(i % tiles_per_seq == 0)
        def _():
            last_s[...] = jnp.zeros_like(last_s)

        row = _iota(cur.shape, 0)
        first = jnp.broadcast_to(last_s[7:8, :], cur.shape)
        prev = jnp.where(row == 0, first, pltpu.roll(cur, 1, 0))
        last_s[...] = cur[cur.shape[0] - 8:]
    mixed = cur + (prev - cur) * mu_ref[...]
    r = mixed[:, 0:d]
    kc = mixed[:, d:2 * d]
    vc = mixed[:, 2 * d:3 * d]
    t1 = mixed[:, 3 * d:3 * d + LANES]
    lane = _iota(t1.shape, 1)
    t1 = jnp.where(lane < 64, jnp.tanh(t1), t1)
    lora = _bdot(t1, w12_ref[...])
    logw = -_softplus(-(w0_ref[...] + lora[:, 0:d])) - 0.5
    a = _sigmoid(a0_ref[...] + lora[:, d:2 * d])
    g = _bdot(_sigmoid(mixed[:, 3 * d + LANES:3 * d + 2 * LANES]), wg2_ref[...])
    kk = kc * kk_ref[...]
    ss = _seg_sum(kk * kk, _seg_ones(RWKV_HEAD_DIM))
    kk = kk * lax.rsqrt(jnp.maximum(ss, 1e-24))
    r_o[...] = r
    lw_o[...] = -jnp.exp(logw)
    k_o[...] = kc * (1.0 + (a - 1.0) * ka_ref[...])
    v_o[...] = vc
    kk_o[...] = kk
    a_o[...] = a
    g_o[...] = g


def _rwkv_prep(cur, prev, mu, w12, wg2, w0, a0, k_k, k_a, *, tm, seq_len):
    t = cur.shape[0]
    tm = min(tm, t)
    prev_given = prev is not None
    row_spec = pl.BlockSpec((tm, RWKV_SHIFT_DIM), lambda i: (i, 0))
    vec = lambda n: pl.BlockSpec((1, n), lambda i: (0, 0))
    in_specs = [row_spec] + ([row_spec] if prev_given else []) + [
        vec(RWKV_SHIFT_DIM),
        pl.BlockSpec((LANES, 2 * D_MODEL), lambda i: (0, 0)),
        pl.BlockSpec((LANES, D_MODEL), lambda i: (0, 0)),
        vec(D_MODEL), vec(D_MODEL), vec(D_MODEL), vec(D_MODEL)]
    args = [cur] + ([prev] if prev_given else []) + [mu[None, :], w12, wg2, w0[None, :], a0[None, :],
                                                    k_k[None, :], k_a[None, :]]
    out_spec = pl.BlockSpec((tm, D_MODEL), lambda i: (i, 0))
    return pl.pallas_call(
        functools.partial(_rwkv_prep_kernel, tiles_per_seq=max(seq_len // tm, 1), prev_given=prev_given),
        grid=(t // tm,),
        in_specs=in_specs,
        out_specs=[out_spec] * 7,
        out_shape=[jax.ShapeDtypeStruct((t, D_MODEL), F32)] * 7,
        scratch_shapes=[] if prev_given else [pltpu.VMEM((8, RWKV_SHIFT_DIM), F32)],
        compiler_params=_cp(("arbitrary",)),
        name="rwkv_prep",
    )(*args)


def _rwkv_kernel(*refs, lb, has_state):
    if has_state:
        (r_ref, lw_ref, k_ref, v_ref, kk_ref, a_ref, g_ref, rk_ref, gw_ref, gb_ref, s0_ref,
         y_ref, so_ref, s_s) = refs
    else:
        (r_ref, lw_ref, k_ref, v_ref, kk_ref, a_ref, g_ref, rk_ref, gw_ref, gb_ref,
         y_ref, so_ref, s_s) = refs
    c = pl.program_id(1)
    qn = RWKV_Q
    hd = RWKV_HEAD_DIM

    @pl.when(c == 0)
    def _():
        if has_state:
            s_s[...] = s0_ref[0]
        else:
            s_s[...] = jnp.zeros_like(s_s)

    ri = _iota((qn, qn), 0)
    ci = _iota((qn, qn), 1)
    lw_all = _pad_rows(lw_ref[0], qn)
    cw_all = _exact_left(ri >= ci, lw_all)
    lane = _iota((qn, LANES), 1)
    low = lane < hd
    rowq = _iota((qn, LANES), 0)
    lane_m = _mod(lane, hd)
    strict = rowq > lane_m
    incl = rowq >= lane_m
    r2 = _iota((2 * qn, LANES), 0)
    l2 = _iota((2 * qn, LANES), 1)
    bd = (r2 < qn) == (l2 < hd)
    strict_bd = bd & (_mod(r2, qn) > _mod(l2, hd))
    eye = (r2 == l2).astype(F32)
    ones_bd = _seg_ones(hd)
    steps = int(math.log2(qn))
    for p in range(RWKV_HEADS // 2):
        sl = slice(p * LANES, (p + 1) * LANES)
        lw = lw_all[:, sl]
        cw = cw_all[:, sl]
        r = _pad_rows(r_ref[0, :, sl], qn)
        k = _pad_rows(k_ref[0, :, sl], qn)
        v = _pad_rows(v_ref[0, :, sl], qn)
        kk = _pad_rows(kk_ref[0, :, sl], qn)
        av = _pad_rows(a_ref[0, :, sl], qn)
        cwl = cw[qn - 1:qn, :]
        e_pos = jnp.exp(cw)
        e_neg = jnp.exp(-cw)
        e_q = jnp.exp(cwl - cw)
        bb = kk * av
        at = -kk * jnp.exp(cw - lw)
        rt = r * e_pos
        lst = jnp.concatenate([jnp.where(low, at, 0.0), jnp.where(low, 0.0, at),
                               jnp.where(low, rt, 0.0), jnp.where(low, 0.0, rt)], axis=0)
        rst = jnp.concatenate([bb * e_neg, k * e_neg], axis=0)
        pm = _bdot_nt(lst, rst)
        p0, p1, p2, p3 = pm[0:qn], pm[qn:2 * qn], pm[2 * qn:3 * qn], pm[3 * qn:4 * qn]
        n_bd = jnp.where(strict_bd, jnp.concatenate([p0, pltpu.roll(p1, hd, 1)], axis=0), 0.0)
        aak = jnp.where(strict, jnp.where(low, pltpu.roll(p0, hd, 1), p1), 0.0)
        arb = jnp.where(incl, jnp.where(low, p2, pltpu.roll(p3, hd, 1)), 0.0)
        ark = jnp.where(incl, jnp.where(low, pltpu.roll(p2, hd, 1), p3), 0.0)
        t_inv = eye + n_bd
        mpow = n_bd
        for _ in range(steps - 1):
            mpow = _bdot(mpow, mpow)
            t_inv = t_inv + _bdot(t_inv, mpow)
        s_bd = s_s[p]
        v_m = jnp.concatenate([jnp.where(low, v, 0.0), jnp.where(low, 0.0, v)], axis=0)
        rhs = _bdot_nt(at, s_bd) + _bdot(aak, v_m)
        rhs_m = jnp.concatenate([jnp.where(low, rhs, 0.0), jnp.where(low, 0.0, rhs)], axis=0)
        u2 = _bdot(t_inv, rhs_m)
        u = u2[0:qn] + u2[qn:2 * qn]
        u_m = jnp.concatenate([jnp.where(low, u, 0.0), jnp.where(low, 0.0, u)], axis=0)
        y = (_bdot_nt(rt, s_bd) + _bdot(jnp.concatenate([arb, ark], axis=1),
                                        jnp.concatenate([u_m, v_m], axis=0)))
        upd = _bdot_tn(jnp.concatenate([u, v], axis=0), jnp.concatenate([bb * e_q, k * e_q], axis=0))
        s_s[p] = s_bd * jnp.exp(cwl) + jnp.where(bd, upd, 0.0)
        y = y[0:lb]
        inv_n = 1.0 / hd
        mu = _seg_sum(y, ones_bd) * inv_n
        yc = y - mu
        var = _seg_sum(yc * yc, ones_bd) * inv_n
        yn = yc * lax.rsqrt(var + RWKV_GN_EPS) * gw_ref[:, sl] + gb_ref[:, sl]
        bonus = _seg_sum(r[0:lb] * k[0:lb] * rk_ref[:, sl], ones_bd) * v[0:lb]
        y_ref[0, :, sl] = (yn + bonus) * g_ref[0, :, sl]
    so_ref[0] = s_s[...]


def _rwkv(r, lw, k, v, kk, a, g, r_k, gn_w, gn_b, s0_bd, *, lb):
    b, l, _ = r.shape
    nc = l // lb
    has_state = s0_bd is not None
    tok = pl.BlockSpec((1, lb, D_MODEL), lambda bi, c: (bi, c, 0))
    vec = pl.BlockSpec((1, D_MODEL), lambda bi, c: (0, 0))
    st = pl.BlockSpec((1, RWKV_HEADS // 2, LANES, LANES), lambda bi, c: (bi, 0, 0, 0))
    in_specs = [tok] * 7 + [vec] * 3 + ([st] if has_state else [])
    args = [r, lw, k, v, kk, a, g, r_k.reshape(1, -1), gn_w[None, :], gn_b[None, :]] + ([s0_bd] if has_state else [])
    return pl.pallas_call(
        functools.partial(_rwkv_kernel, lb=lb, has_state=has_state),
        grid=(b, nc),
        in_specs=in_specs,
        out_specs=[tok, st],
        out_shape=[jax.ShapeDtypeStruct((b, l, D_MODEL), F32),
                   jax.ShapeDtypeStruct((b, RWKV_HEADS // 2, LANES, LANES), F32)],
        scratch_shapes=[pltpu.VMEM((RWKV_HEADS // 2, LANES, LANES), F32)],
        compiler_params=_cp(("parallel", "arbitrary")),
        name="rwkv",
    )(*args)


def _memattn_kernel(q_ref, mk_ref, mv_ref, o_ref):
    scale = MEM_HEAD_DIM ** -0.5
    for h in range(MEM_HEADS):
        sl = slice(h * MEM_HEAD_DIM, (h + 1) * MEM_HEAD_DIM)
        s = _bdot_nt(q_ref[0, :, sl], mk_ref[0, 0, :, sl]) * scale
        m = jnp.max(s, axis=-1, keepdims=True)
        p = jnp.exp(s - m)
        l = jnp.sum(p, axis=-1, keepdims=True)
        o_ref[0, :, sl] = _bdot(p / l, mv_ref[0, 0, :, sl])


def _mem_attn(q, mk, mv, layer, *, tq):
    b, l, _ = q.shape
    tq = min(tq, l)
    return pl.pallas_call(
        _memattn_kernel,
        grid=(b, l // tq),
        in_specs=[pl.BlockSpec((1, tq, D_MODEL), lambda bi, i: (bi, i, 0)),
                  pl.BlockSpec((1, 1, N_MEM, D_MODEL), lambda bi, i: (layer, bi, 0, 0)),
                  pl.BlockSpec((1, 1, N_MEM, D_MODEL), lambda bi, i: (layer, bi, 0, 0))],
        out_specs=pl.BlockSpec((1, tq, D_MODEL), lambda bi, i: (bi, i, 0)),
        out_shape=jax.ShapeDtypeStruct((b, l, D_MODEL), F32),
        compiler_params=_cp(("parallel", "arbitrary")),
        name="mem_attn",
    )(q, mk, mv)


def _pad_tokens(x, lp):
    b, l = x.shape[0], x.shape[1]
    if l == lp:
        return x
    return jnp.pad(x, ((0, 0), (0, lp - l)) + ((0, 0),) * (x.ndim - 2))


def _even_layer(x, w, e, grp):
    b, l = grp["b"], grp["l"]
    sample = grp["sample"]
    gain = w["norm_mix"][2 * e]
    pm = _mm([x], [w["e_w_main"][e]], gain=gain, name="e_in_main")
    v = _mm([x], [w["e_w_v"][e]], gain=gain, name="e_in_v")
    dt, dtt = _dt_proj(x, gain, w["e_w_dt"][e], w["e_dt_bias"][e])
    hg = SSD_HEADS // SSD_GROUPS
    lp = l if not sample else 8
    lq = max(lp, SSD_Q)
    dt4 = dt[:, :SSD_HEADS].reshape(b, l, SSD_GROUPS, hg).transpose(0, 2, 1, 3)
    dt4 = jnp.pad(dt4, ((0, 0), (0, 0), (0, lq - l), (0, LANES - hg)))
    dtt4 = dtt.reshape(SSD_GROUPS, hg, b, l).transpose(2, 0, 1, 3)
    dtt4 = jnp.pad(dtt4, ((0, 0), (0, 0), (0, 0), (0, lq - l)))
    pm3 = _pad_tokens(pm.reshape(b, l, -1), lp)
    xbc = pm[:, SSD_INNER:SSD_INNER + XBC_DIM].reshape(b, l, XBC_DIM)
    if sample:
        conv_prev = grp["conv"][e]
        s0 = grp["ssd"][e].reshape(b, SSD_HEADS // 2, LANES, LANES)
        conv_new = jnp.concatenate([conv_prev, xbc], axis=1)[:, -(SSD_CONV - 1):]
    else:
        conv_prev, s0 = None, None
        conv_new = xbc[:, -(SSD_CONV - 1):]
    y_ssd, s_new = _ssd(pm3, dt4, dtt4, w["e_conv_w"][e], w["e_conv_b"][e], w["e_a_log"][e], w["e_d_skip"][e],
                        w["e_g_ssd"][e], conv_prev, s0, lb=min(lp, SSD_Q))
    y_ssd = y_ssd[:, :l].reshape(b * l, SSD_INNER)
    s_new = s_new.reshape(b, SSD_HEADS, SSD_HEAD_DIM, SSD_STATE)
    q0 = grp["q0"]
    cos_t, sin_t = _rope_tables(q0 + jnp.arange(l))
    if sample:
        cos_t, sin_t = jnp.tile(cos_t, (b, 1)), jnp.tile(sin_t, (b, 1))
        q_r, k_r = _qk_prep(pm, 5, 6, cos_t, sin_t, w["e_g_q"][e], w["e_g_k"][e], tq=b * l, with_kmean=False)
        pad8 = lambda t: _pad_tokens(t.reshape(b, l, D_MODEL), 8)
        o = _moba_sample(grp["page_table"], pad8(q_r), pad8(k_r), pad8(v), grp["cache_k"], grp["cache_v"],
                         layer_off=e * grp["n_pool"], l_real=l)[:, :l].reshape(b * l, D_MODEL)
    else:
        q_r, k_r, kmean = _qk_prep(pm, 5, 6, cos_t, sin_t, w["e_g_q"][e], w["e_g_k"][e], tq=MOBA_BLOCK,
                                   with_kmean=True)
        o = _moba_prompt(q_r, k_r, v, kmean.reshape(b, l // MOBA_BLOCK, D_MODEL), b=b, l=l)
    x = _mm([y_ssd, o], [w["e_w_out_a"][e], w["e_w_out_b"][e]], res=x, tm=512, name="e_out")
    hd = (b, l, MOBA_HEADS, MOBA_HEAD_DIM)
    return x, s_new, conv_new, k_r.reshape(hd), v.reshape(hd)


def _odd_layer(x, w, o, grp):
    b, l = grp["b"], grp["l"]
    sample = grp["sample"]
    gain = w["norm_mix"][2 * o + 1]
    cur = _mm([x], [w["o_w_cur"][o]], gain=gain, tn=1664, name="o_in_cur")
    pg = _mm([x], [w["o_w_g"][o]], gain=gain, name="o_in_g")
    gf = _mm([x], [w["o_w_gf"][o]], gain=gain, name="o_in_gf")
    cur3 = cur.reshape(b, l, RWKV_SHIFT_DIM)
    if sample:
        prev = jnp.concatenate([grp["shift"][o][:, None], cur3[:, :-1]], axis=1).reshape(b * l, RWKV_SHIFT_DIM)
    else:
        prev = None
    vecs = _rwkv_prep(cur, prev, w["o_mu"][o], w["o_w12"][o], w["o_w_g2"][o], w["o_w0"][o], w["o_a0"][o],
                      w["o_k_k"][o], w["o_k_a"][o], tm=512, seq_len=l)
    lp = l if not sample else 8
    vecs3 = [_pad_tokens(t.reshape(b, l, D_MODEL), lp) for t in vecs]
    if sample:
        s0 = grp["rwkv"][o].reshape(b, RWKV_HEADS // 2, 2, RWKV_HEAD_DIM, RWKV_HEAD_DIM)
        z = jnp.zeros_like(s0[:, :, 0])
        s0_bd = jnp.concatenate([jnp.concatenate([s0[:, :, 0], z], axis=-1),
                                 jnp.concatenate([z, s0[:, :, 1]], axis=-1)], axis=-2)
    else:
        s0_bd = None
    y_r, s_bd = _rwkv(*vecs3, w["o_r_k"][o], w["o_gn_w"][o], w["o_gn_b"][o], s0_bd, lb=min(lp, RWKV_Q))
    y_r = y_r[:, :l].reshape(b * l, D_MODEL)
    hd = RWKV_HEAD_DIM
    s_r = jnp.stack([s_bd[:, :, :hd, :hd], s_bd[:, :, hd:, hd:]], axis=2).reshape(b, RWKV_HEADS, hd, hd)
    pg3 = _pad_tokens(pg.reshape(b, l, -1), lp)
    gf3 = _pad_tokens(gf.reshape(b, l, LANES), lp)
    og, s_g = _gla(pg3, gf3, w["o_w_f2"][o], w["o_b_f"][o], w["o_g_gla"][o], grp["gla"][o] if sample else None,
                   lb=min(lp, GLA_Q), n_valid=min(l, GLA_Q))
    og = og[:, :l].reshape(b * l, GLA_HEADS * GLA_DV)
    x = _mm([y_r, og], [w["o_w_out_a"][o], w["o_w_out_b"][o]], res=x, tm=512, name="o_out")
    return x, s_r, cur3[:, -1], s_g


def _trunk(x, w, grp):
    b, l = grp["b"], grp["l"]
    outs = {k: [] for k in ("ssd", "conv", "k", "v", "rwkv", "shift", "gla")}
    for layer in range(DEPTH):
        if layer % 2 == 0:
            x, s, cnew, k, v = _even_layer(x, w, layer // 2, grp)
            outs["ssd"].append(s)
            outs["conv"].append(cnew)
            outs["k"].append(k)
            outs["v"].append(v)
        else:
            x, s_r, sh, s_g = _odd_layer(x, w, layer // 2, grp)
            outs["rwkv"].append(s_r)
            outs["shift"].append(sh)
            outs["gla"].append(s_g)
        qm = _mm([x], [w["m_w_q"][layer]], gain=w["norm_mem"][layer], head_gain=w["m_g_q"][layer], name="mem_q")
        lp = 8 if grp["sample"] else l
        om = _mem_attn(_pad_tokens(qm.reshape(b, l, D_MODEL), lp), grp["mem_k"], grp["mem_v"], layer, tq=512)
        x = _mm([om[:, :l].reshape(b * l, D_MODEL)], [w["m_w_o"][layer]], res=x, name="mem_o")
        x = _ffn(x, w["norm_ffn"][layer], w["f_w_g"][layer], w["f_w_u"][layer], w["f_w_d"][layer])
    return x, {k: jnp.stack(v) for k, v in outs.items()}


def kernel(x_prompt, x_sample, cache_moba_k, cache_moba_v, state_ssd, state_ssd_conv, state_rwkv, state_rwkv_shift, state_gla, cache_mem_k, cache_mem_v, page_table, mem_prompt, norm_mix, norm_mem, norm_memtok, norm_ffn, e_w_in, e_conv_w, e_conv_b, e_dt_bias, e_a_log, e_d_skip, e_g_ssd, e_g_q, e_g_k, e_w_out, o_w_in, o_mu, o_w0, o_w_w2, o_a0, o_w_a2, o_w_g2, o_k_k, o_k_a, o_r_k, o_gn_w, o_gn_b, o_w_f2, o_b_f, o_g_gla, o_w_out, m_w_q, m_w_kv, m_g_q, m_g_k, m_w_o, f_w_gu, f_w_down):
    bp, lp, _ = x_prompt.shape
    bs, ls, _ = x_sample.shape
    n_pool = cache_moba_k.shape[1]
    n_pages = page_table.shape[1]
    c_dt = SSD_INNER + XBC_DIM
    c_q = c_dt + SSD_HEADS
    c_cur = RWKV_SHIFT_DIM
    z64 = jnp.zeros((N_ODD, 64, D_MODEL), F32)
    w = dict(
        norm_mix=norm_mix, norm_mem=norm_mem, norm_ffn=norm_ffn,
        e_w_main=jnp.concatenate([e_w_in[:, :, :c_dt], e_w_in[:, :, c_q:c_q + 2 * D_MODEL]], axis=-1).astype(BF16),
        e_w_v=e_w_in[:, :, c_q + 2 * D_MODEL:].astype(BF16),
        e_w_dt=e_w_in[:, :, c_dt:c_q],
        e_conv_w=e_conv_w, e_conv_b=e_conv_b, e_dt_bias=e_dt_bias, e_a_log=e_a_log, e_d_skip=e_d_skip,
        e_g_ssd=e_g_ssd, e_g_q=e_g_q, e_g_k=e_g_k,
        e_w_out_a=e_w_out[:, :SSD_INNER].astype(BF16), e_w_out_b=e_w_out[:, SSD_INNER:].astype(BF16),
        o_w_cur=o_w_in[:, :, :c_cur].astype(BF16),
        o_w_g=jnp.concatenate([o_w_in[:, :, c_cur + 2 * GLA_KEY_DIM:c_cur + 2 * GLA_KEY_DIM + 2 * D_MODEL],
                               o_w_in[:, :, c_cur:c_cur + 2 * GLA_KEY_DIM]], axis=-1).astype(BF16),
        o_w_gf=jnp.pad(o_w_in[:, :, c_cur + 2 * GLA_KEY_DIM + 2 * D_MODEL:],
                       ((0, 0), (0, 0), (0, LANES - GLA_GATE_RANK))).astype(BF16),
        o_mu=o_mu, o_w0=o_w0, o_a0=o_a0, o_k_k=o_k_k, o_k_a=o_k_a, o_r_k=o_r_k, o_gn_w=o_gn_w, o_gn_b=o_gn_b,
        o_w12=jnp.concatenate([jnp.concatenate([o_w_w2, z64], axis=-1),
                               jnp.concatenate([z64, o_w_a2], axis=-1)], axis=1).astype(BF16),
        o_w_g2=o_w_g2.astype(BF16), o_w_f2=o_w_f2, o_b_f=o_b_f, o_g_gla=o_g_gla,
        o_w_out_a=o_w_out[:, :D_MODEL].astype(BF16), o_w_out_b=o_w_out[:, D_MODEL:].astype(BF16),
        m_w_q=m_w_q.astype(BF16), m_g_q=m_g_q, m_w_o=m_w_o.astype(BF16),
        f_w_g=f_w_gu[:, :, :D_FF].astype(BF16), f_w_u=f_w_gu[:, :, D_FF:].astype(BF16),
        f_w_d=f_w_down.astype(BF16),
    )
    mem2 = mem_prompt.reshape(bp * N_MEM, D_MODEL)
    mks, mvs = [], []
    for layer in range(DEPTH):
        wkv = m_w_kv[layer].astype(BF16)
        mks.append(_mm([mem2], [wkv[:, :D_MODEL]], gain=norm_memtok[layer], head_gain=m_g_k[layer], name="mem_k"))
        mvs.append(_mm([mem2], [wkv[:, D_MODEL:]], gain=norm_memtok[layer], name="mem_v"))
    mem_k_p = jnp.stack(mks).reshape(DEPTH, bp, N_MEM, D_MODEL)
    mem_v_p = jnp.stack(mvs).reshape(DEPTH, bp, N_MEM, D_MODEL)

    grp_p = dict(b=bp, l=lp, sample=False, q0=0, mem_k=mem_k_p, mem_v=mem_v_p)
    y_p, o_p = _trunk(x_prompt.reshape(bp * lp, D_MODEL), w, grp_p)

    grp_s = dict(b=bs, l=ls, sample=True, q0=n_pages * PAGE_SIZE, page_table=page_table, n_pool=n_pool,
                 cache_k=cache_moba_k.reshape(N_EVEN * n_pool, PAGE_SIZE, D_MODEL),
                 cache_v=cache_moba_v.reshape(N_EVEN * n_pool, PAGE_SIZE, D_MODEL),
                 ssd=state_ssd, conv=state_ssd_conv, rwkv=state_rwkv, shift=state_rwkv_shift, gla=state_gla,
                 mem_k=cache_mem_k.reshape(DEPTH, bs, N_MEM, D_MODEL),
                 mem_v=cache_mem_v.reshape(DEPTH, bs, N_MEM, D_MODEL))
    y_s, o_s = _trunk(x_sample.reshape(bs * ls, D_MODEL), w, grp_s)

    mem_shape = (DEPTH, bp, N_MEM, MEM_HEADS, MEM_HEAD_DIM)
    return (y_p.reshape(bp, lp, D_MODEL), y_s.reshape(bs, ls, D_MODEL),
            o_p["k"], o_p["v"], o_s["k"], o_s["v"], o_p["ssd"], o_s["ssd"], o_p["conv"], o_s["conv"],
            o_p["rwkv"], o_s["rwkv"], o_p["shift"], o_s["shift"], o_p["gla"], o_s["gla"],
            mem_k_p.reshape(mem_shape), mem_v_p.reshape(mem_shape))
```

```python
import functools
import math

import jax
import jax.numpy as jnp
from jax import lax
from jax.experimental import pallas as pl
from jax.experimental.pallas import tpu as pltpu

F32 = jnp.float32
BF16 = jnp.bfloat16

D_MODEL = 1024
DEPTH = 4
N_EVEN = 2
N_ODD = 2
NORM_EPS = 1e-6
PAGE_SIZE = 128

SSD_INNER = 2048
SSD_HEAD_DIM = 64
SSD_HEADS = 32
SSD_GROUPS = 4
SSD_STATE = 128
SSD_CONV = 4
XBC_DIM = SSD_INNER + 2 * SSD_GROUPS * SSD_STATE
SSD_Q = 128

MOBA_HEADS = 8
MOBA_HEAD_DIM = 128
MOBA_BLOCK = 256
MOBA_TOPK = 3
ROT_DIM = 32
ROPE_THETA = 500000.0

RWKV_HEAD_DIM = 64
RWKV_HEADS = 16
RWKV_GN_EPS = 64e-5
RWKV_SHIFT_DIM = 3 * D_MODEL + 64 + 64 + 128
RWKV_Q = 64

GLA_HEADS = 4
GLA_DK = 128
GLA_DV = 256
GLA_KEY_DIM = 512
GLA_GATE_RANK = 16
GLA_GATE_NORM = 16.0
GLA_Q = 64

N_MEM = 256
MEM_HEADS = 4
MEM_HEAD_DIM = 256
D_FF = 2816

LANES = 128
SUBLANES = 8
VMEM_LIMIT = 56 * 1024 * 1024


def _cp(sem, vmem=VMEM_LIMIT):
    return pltpu.CompilerParams(dimension_semantics=sem, vmem_limit_bytes=vmem)


def _bdot(a, b):
    return jnp.dot(a.astype(BF16), b.astype(BF16), preferred_element_type=F32)


def _bdot_nt(a, b):
    return lax.dot_general(a.astype(BF16), b.astype(BF16), (((1,), (1,)), ((), ())),
                           preferred_element_type=F32)


def _bdot_tn(a, b):
    k = a.shape[0]
    kp = -(-k // LANES) * LANES
    if kp != k:
        a = jnp.concatenate([a, jnp.zeros((kp - k, a.shape[1]), a.dtype)], axis=0)
        b = jnp.concatenate([b, jnp.zeros((kp - k, b.shape[1]), b.dtype)], axis=0)
    return jnp.dot(a.T.astype(BF16), b.astype(BF16), preferred_element_type=F32)


def _split3(x):
    hi = x.astype(BF16)
    r1 = x - hi.astype(F32)
    mid = r1.astype(BF16)
    lo = (r1 - mid.astype(F32)).astype(BF16)
    return hi, mid, lo


def _exact_left(m01, x):
    hi, mid, lo = _split3(x)
    m = m01.astype(BF16)
    return (jnp.dot(m, hi, preferred_element_type=F32) + jnp.dot(m, mid, preferred_element_type=F32)
            + jnp.dot(m, lo, preferred_element_type=F32))


def _exact_right(x, m01):
    hi, mid, lo = _split3(x)
    m = m01.astype(BF16)
    return (jnp.dot(hi, m, preferred_element_type=F32) + jnp.dot(mid, m, preferred_element_type=F32)
            + jnp.dot(lo, m, preferred_element_type=F32))


def _f32_nt(a, b):
    ah, am, al = _split3(a)
    bh, bm, bl = _split3(b)
    dn = (((1,), (1,)), ((), ()))
    d = lambda x, y: lax.dot_general(x, y, dn, preferred_element_type=F32)
    return d(ah, bh) + (d(ah, bm) + d(am, bh)) + (d(am, bm) + d(ah, bl) + d(al, bh))


def _rms(x, g, eps=NORM_EPS):
    return x * lax.rsqrt(jnp.mean(x * x, axis=-1, keepdims=True) + eps) * g


def _sigmoid(x):
    return 1.0 / (1.0 + jnp.exp(-x))


def _silu(x):
    return x * _sigmoid(x)


def _softplus(x):
    return jnp.maximum(x, 0.0) + jnp.log1p(jnp.exp(-jnp.abs(x)))


def _iota(shape, dim):
    return lax.broadcasted_iota(jnp.int32, shape, dim)


def _div(x, d):
    return lax.shift_right_logical(x, jnp.int32(int(math.log2(d))))


def _mod(x, d):
    return x & jnp.int32(d - 1)


def _pad_rows(x, rows):
    if x.shape[0] == rows:
        return x
    return jnp.concatenate([x, jnp.zeros((rows - x.shape[0], x.shape[1]), x.dtype)], axis=0)


def _seg_ones(seg):
    r = _div(_iota((LANES, LANES), 0), seg)
    c = _div(_iota((LANES, LANES), 1), seg)
    return (r == c).astype(BF16)


def _seg_sum(x, ones_bd):
    parts = []
    for s in range(x.shape[1] // LANES):
        parts.append(_exact_right(x[:, s * LANES:(s + 1) * LANES], ones_bd))
    return parts[0] if len(parts) == 1 else jnp.concatenate(parts, axis=1)


def _mm_kernel(*refs, n_a, has_gain, hn_dim, has_res):
    pos = 0
    a_refs = refs[pos:pos + n_a]
    pos += n_a
    g_ref = None
    if has_gain:
        g_ref = refs[pos]
        pos += 1
    w_refs = refs[pos:pos + n_a]
    pos += n_a
    hn_ref = None
    if hn_dim:
        hn_ref = refs[pos]
        pos += 1
    res_ref = None
    if has_res:
        res_ref = refs[pos]
        pos += 1
    o_ref = refs[pos]
    pos += 1
    if has_gain:
        an_ref = refs[pos]

        @pl.when(pl.program_id(1) == 0)
        def _():
            an_ref[...] = _rms(a_refs[0][...].astype(F32), g_ref[...]).astype(BF16)

        acc = jnp.dot(an_ref[...], w_refs[0][...], preferred_element_type=F32)
    else:
        acc = None
        for a_ref, w_ref in zip(a_refs, w_refs):
            d = jnp.dot(a_ref[...].astype(BF16), w_ref[...], preferred_element_type=F32)
            acc = d if acc is None else acc + d
    if hn_dim:
        parts = [_rms(acc[:, s * hn_dim:(s + 1) * hn_dim], hn_ref[...]) for s in range(acc.shape[1] // hn_dim)]
        acc = parts[0] if len(parts) == 1 else jnp.concatenate(parts, axis=1)
    if has_res:
        acc = acc + res_ref[...]
    o_ref[...] = acc.astype(o_ref.dtype)


def _mm(a_list, w_list, *, gain=None, head_gain=None, res=None, tm=1024, tn=1024, out_dtype=F32, name="mm"):
    m = a_list[0].shape[0]
    n = w_list[0].shape[1]
    tm = min(tm, m)
    tn = min(tn, n)
    assert m % tm == 0 and n % tn == 0, (m, tm, n, tn)
    has_gain = gain is not None
    assert not has_gain or len(a_list) == 1
    in_specs = [pl.BlockSpec((tm, a.shape[1]), lambda i, j: (i, 0)) for a in a_list]
    args = list(a_list)
    if has_gain:
        in_specs.append(pl.BlockSpec((1, a_list[0].shape[1]), lambda i, j: (0, 0)))
        args.append(gain.reshape(1, -1))
    for w in w_list:
        in_specs.append(pl.BlockSpec((w.shape[0], tn), lambda i, j: (0, j)))
        args.append(w)
    hn_dim = 0
    if head_gain is not None:
        hn_dim = head_gain.shape[-1]
        in_specs.append(pl.BlockSpec((1, hn_dim), lambda i, j: (0, 0)))
        args.append(head_gain.reshape(1, -1))
    if res is not None:
        in_specs.append(pl.BlockSpec((tm, tn), lambda i, j: (i, j)))
        args.append(res)
    scratch = [pltpu.VMEM((tm, a_list[0].shape[1]), BF16)] if has_gain else []
    return pl.pallas_call(
        functools.partial(_mm_kernel, n_a=len(a_list), has_gain=has_gain, hn_dim=hn_dim, has_res=res is not None),
        grid=(m // tm, n // tn),
        in_specs=in_specs,
        out_specs=pl.BlockSpec((tm, tn), lambda i, j: (i, j)),
        out_shape=jax.ShapeDtypeStruct((m, n), out_dtype),
        scratch_shapes=scratch,
        compiler_params=_cp(("parallel", "arbitrary")),
        name=name,
    )(*args)


def _ffn_kernel(x_ref, g_ref, wg_ref, wu_ref, wd_ref, o_ref, an_ref, acc_ref):
    f = pl.program_id(1)

    @pl.when(f == 0)
    def _():
        an_ref[...] = _rms(x_ref[...], g_ref[...]).astype(BF16)
        acc_ref[...] = jnp.zeros_like(acc_ref)

    a = an_ref[...]
    g = jnp.dot(a, wg_ref[...], preferred_element_type=F32)
    u = jnp.dot(a, wu_ref[...], preferred_element_type=F32)
    act = (_silu(g) * u).astype(BF16)
    acc_ref[...] += jnp.dot(act, wd_ref[...], preferred_element_type=F32)

    @pl.when(f == pl.num_programs(1) - 1)
    def _():
        o_ref[...] = x_ref[...] + acc_ref[...]


def _ffn(x, gain, wg, wu, wd, *, tm=1024, tf=256):
    m = x.shape[0]
    tm = min(tm, m)
    nf = D_FF // tf
    return pl.pallas_call(
        _ffn_kernel,
        grid=(m // tm, nf),
        in_specs=[pl.BlockSpec((tm, D_MODEL), lambda i, f: (i, 0)),
                  pl.BlockSpec((1, D_MODEL), lambda i, f: (0, 0)),
                  pl.BlockSpec((D_MODEL, tf), lambda i, f: (0, f)),
                  pl.BlockSpec((D_MODEL, tf), lambda i, f: (0, f)),
                  pl.BlockSpec((tf, D_MODEL), lambda i, f: (f, 0))],
        out_specs=pl.BlockSpec((tm, D_MODEL), lambda i, f: (i, 0)),
        out_shape=jax.ShapeDtypeStruct((m, D_MODEL), F32),
        scratch_shapes=[pltpu.VMEM((tm, D_MODEL), BF16), pltpu.VMEM((tm, D_MODEL), F32)],
        compiler_params=_cp(("parallel", "arbitrary")),
        name="ffn",
    )(x, gain.reshape(1, -1), wg, wu, wd)


def _qkprep_kernel(q_ref, k_ref, cos_ref, sin_ref, gq_ref, gk_ref, qo_ref, ko_ref, *maybe_km, with_kmean):
    cos = cos_ref[...]
    sin = sin_ref[...]
    lane = _iota(cos.shape, 1)
    half = ROT_DIM // 2

    def prep(x_ref, g_ref, o_ref):
        for h in range(MOBA_HEADS):
            sl = slice(h * MOBA_HEAD_DIM, (h + 1) * MOBA_HEAD_DIM)
            y = _rms(x_ref[:, sl], g_ref[...])
            sw = jnp.where(lane < half, pltpu.roll(y, MOBA_HEAD_DIM - half, 1), pltpu.roll(y, half, 1))
            o_ref[:, sl] = y * cos + sw * sin

    prep(q_ref, gq_ref, qo_ref)
    prep(k_ref, gk_ref, ko_ref)
    if with_kmean:
        maybe_km[0][0] = jnp.mean(ko_ref[...], axis=0, keepdims=True)


def _rope_tables(pos):
    half = ROT_DIM // 2
    inv_freq = ROPE_THETA ** (-jnp.arange(half, dtype=F32) / half)
    ang = pos.astype(F32)[:, None] * inv_freq[None, :]
    cos, sin = jnp.cos(ang), jnp.sin(ang)
    ones = jnp.ones((pos.shape[0], MOBA_HEAD_DIM - ROT_DIM), F32)
    cos_t = jnp.concatenate([cos, cos, ones], axis=1)
    sin_t = jnp.concatenate([-sin, sin, 0.0 * ones], axis=1)
    return cos_t, sin_t


def _qk_prep(proj, q_col, k_col, cos_t, sin_t, gq, gk, *, tq, with_kmean):
    t = proj.shape[0]
    n_tab = cos_t.shape[0] // tq
    out_shape = [jax.ShapeDtypeStruct((t, D_MODEL), F32), jax.ShapeDtypeStruct((t, D_MODEL), F32)]
    out_specs = [pl.BlockSpec((tq, D_MODEL), lambda i: (i, 0)), pl.BlockSpec((tq, D_MODEL), lambda i: (i, 0))]
    if with_kmean:
        out_shape.append(jax.ShapeDtypeStruct((t // tq, 1, D_MODEL), F32))
        out_specs.append(pl.BlockSpec((1, 1, D_MODEL), lambda i: (i, 0, 0)))
    return pl.pallas_call(
        functools.partial(_qkprep_kernel, with_kmean=with_kmean),
        grid=(t // tq,),
        in_specs=[pl.BlockSpec((tq, D_MODEL), lambda i: (i, q_col)),
                  pl.BlockSpec((tq, D_MODEL), lambda i: (i, k_col)),
                  pl.BlockSpec((tq, MOBA_HEAD_DIM), lambda i: (i % n_tab, 0)),
                  pl.BlockSpec((tq, MOBA_HEAD_DIM), lambda i: (i % n_tab, 0)),
                  pl.BlockSpec((1, MOBA_HEAD_DIM), lambda i: (0, 0)),
                  pl.BlockSpec((1, MOBA_HEAD_DIM), lambda i: (0, 0))],
        out_specs=out_specs,
        out_shape=out_shape,
        compiler_params=_cp(("parallel",)),
        name="qk_prep",
    )(proj, proj, cos_t, sin_t, gq.reshape(1, -1), gk.reshape(1, -1))


def _moba_prompt_kernel(q_ref, k_ref, v_ref, km_ref, o_ref, *, nb, n_sel):
    qi = pl.program_id(2)
    blk = MOBA_BLOCK
    scale = MOBA_HEAD_DIM ** -0.5
    q = q_ref[...]
    qb = q.astype(BF16)
    gt = _f32_nt(_pad_rows(km_ref[0], LANES), q)[0:SUBLANES]
    rown = _iota((SUBLANES, blk), 0)
    past = rown < qi
    sel_t = jnp.zeros((SUBLANES, blk), F32)
    for n in range(nb):
        g_n = gt[n:n + 1, :]
        beats = past & ((gt > g_n) | ((gt == g_n) & (rown < n)))
        sel_n = (jnp.sum(beats.astype(F32), axis=0, keepdims=True) < n_sel).astype(F32)
        sel_t = jnp.where(rown == n, sel_n, sel_t)
    sel_m = _pad_rows(sel_t, LANES).T
    ci = _iota((blk, blk), 1)
    rw = _iota((blk, LANES), 0)
    cn = _iota((blk, LANES), 1)
    thr = jnp.where(cn == qi, rw, jnp.where((sel_m > 0.5) & (cn < qi), blk - 1, -1))

    def attend(nblocks):
        s = _bdot_nt(qb, k_ref[0:nblocks * blk, :]) * scale
        parts = []
        for n in range(nblocks):
            parts.append(jnp.where(ci <= thr[:, n:n + 1], s[:, n * blk:(n + 1) * blk], -jnp.inf))
        s = parts[0] if nblocks == 1 else jnp.concatenate(parts, axis=1)
        m = jnp.max(s, axis=-1, keepdims=True)
        p = jnp.exp(s - m)
        l = jnp.sum(p, axis=-1, keepdims=True)
        o_ref[...] = _bdot(p, v_ref[0:nblocks * blk, :]) / l

    half = nb // 2
    if half >= 1:
        @pl.when(qi < half)
        def _():
            attend(half)

        @pl.when(qi >= half)
        def _():
            attend(nb)
    else:
        attend(nb)


def _moba_prompt(q, k, v, kmean, *, b, l):
    nb = l // MOBA_BLOCK
    assert 1 <= nb <= SUBLANES
    n_sel = min(MOBA_TOPK, (l - 1) // MOBA_BLOCK)
    hd = MOBA_HEAD_DIM
    return pl.pallas_call(
        functools.partial(_moba_prompt_kernel, nb=nb, n_sel=n_sel),
        grid=(b, MOBA_HEADS, nb),
        in_specs=[pl.BlockSpec((MOBA_BLOCK, hd), lambda bi, h, qi: (bi * nb + qi, h)),
                  pl.BlockSpec((l, hd), lambda bi, h, qi: (bi, h)),
                  pl.BlockSpec((l, hd), lambda bi, h, qi: (bi, h)),
                  pl.BlockSpec((1, nb, hd), lambda bi, h, qi: (bi, 0, h))],
        out_specs=pl.BlockSpec((MOBA_BLOCK, hd), lambda bi, h, qi: (bi * nb + qi, h)),
        out_shape=jax.ShapeDtypeStruct((b * l, D_MODEL), F32),
        compiler_params=_cp(("parallel", "parallel", "arbitrary")),
        name="moba_prompt",
    )(q, k, v, kmean)


def _moba_sample_kernel(pt_ref, q_ref, kn_ref, vn_ref, *rest, n_pages, pps, l_real, n_sel):
    del pt_ref
    kc_refs, vc_refs = rest[0:pps], rest[pps:2 * pps]
    o_ref, m_s, l_s, acc_s, ks_s = rest[2 * pps:]
    step = pl.program_id(1)
    nh = MOBA_HEADS
    rows = l_real * nh
    scale = MOBA_HEAD_DIM ** -0.5
    pages_per_block = MOBA_BLOCK // PAGE_SIZE
    q = q_ref[0]
    width = PAGE_SIZE * nh
    same_head = _mod(_iota((rows, width), 0), nh) == _mod(_iota((rows, width), 1), nh)
    kps = [r[0] for r in kc_refs]
    ss = [jnp.where(same_head, _bdot_nt(q, kp) * scale, -jnp.inf) for kp in kps]
    ms = [jnp.max(s, axis=-1, keepdims=True) for s in ss]
    prs = [jnp.exp(s - m) for s, m in zip(ss, ms)]
    accs = [_bdot(pr, r[0]) for pr, r in zip(prs, vc_refs)]
    ksums = [jnp.sum(kp.reshape(PAGE_SIZE, nh, MOBA_HEAD_DIM), axis=0) for kp in kps]
    for i in range(pps):
        pg = step * pps + i
        m_s[pg] = ms[i]
        l_s[pg] = jnp.sum(prs[i], axis=-1, keepdims=True)
        acc_s[pg] = accs[i]
    for j in range(pps // pages_per_block):
        tot = ksums[j * pages_per_block]
        for i in range(1, pages_per_block):
            tot = tot + ksums[j * pages_per_block + i]
        ks_s[step * (pps // pages_per_block) + j] = tot

    @pl.when(step == n_pages // pps - 1)
    def _():
        n_blocks = n_pages // pages_per_block
        gates = []
        for n in range(n_blocks):
            kmean = ks_s[n] * (1.0 / MOBA_BLOCK)
            gates.append(jnp.sum(q * jnp.concatenate([kmean] * l_real, axis=0), axis=-1, keepdims=True))
        sels = []
        for n in range(n_blocks):
            rank = jnp.zeros((rows, 1), F32)
            for j in range(n_blocks):
                if j != n:
                    beats = (gates[j] > gates[n]) | ((gates[j] == gates[n]) & (j < n))
                    rank = rank + beats.astype(F32)
            sels.append(rank < n_sel)
        kn = _pad_rows(kn_ref[0], LANES)
        vn = _pad_rows(vn_ref[0], LANES)
        ro = _iota((rows, LANES), 0)
        co = _iota((rows, LANES), 1)
        vis = (_mod(ro, nh) == _mod(co, nh)) & (_div(co, nh) <= _div(ro, nh))
        s_own = jnp.where(vis, _bdot_nt(q, kn) * scale, -jnp.inf)
        big = jnp.max(s_own, axis=-1, keepdims=True)
        for pg in range(n_pages):
            big = jnp.maximum(big, jnp.where(sels[pg // pages_per_block], m_s[pg], -jnp.inf))
        p_own = jnp.exp(s_own - big)
        den = jnp.sum(p_own, axis=-1, keepdims=True)
        num = _bdot(p_own, vn)
        for pg in range(n_pages):
            w = jnp.where(sels[pg // pages_per_block], jnp.exp(m_s[pg] - big), 0.0)
            den = den + w * l_s[pg]
            num = num + w * acc_s[pg]
        o_ref[0] = num / den


def _moba_sample(page_table, q, k_new, v_new, cache_k, cache_v, *, layer_off):
    b, n_pages = page_table.shape
    rows = q.shape[1]
    l_real = rows // MOBA_HEADS
    q0 = n_pages * PAGE_SIZE
    assert q0 % MOBA_BLOCK == 0 and rows <= LANES
    n_sel = min(MOBA_TOPK, (q0 + l_real - 1) // MOBA_BLOCK)
    hd = MOBA_HEAD_DIM
    pps = 4
    assert n_pages % pps == 0 and pps % (MOBA_BLOCK // PAGE_SIZE) == 0
    tok_spec = pl.BlockSpec((1, rows, hd), lambda bi, p, pt: (bi, 0, 0))

    def page_spec(i):
        return pl.BlockSpec((1, PAGE_SIZE * MOBA_HEADS, hd),
                            lambda bi, p, pt: (layer_off + pt[bi, p * pps + i], 0, 0))

    grid_spec = pltpu.PrefetchScalarGridSpec(
        num_scalar_prefetch=1,
        grid=(b, n_pages // pps),
        in_specs=([tok_spec, tok_spec, tok_spec] + [page_spec(i) for i in range(pps)]
                  + [page_spec(i) for i in range(pps)]),
        out_specs=tok_spec,
        scratch_shapes=[pltpu.VMEM((n_pages, rows, 1), F32),
                        pltpu.VMEM((n_pages, rows, 1), F32),
                        pltpu.VMEM((n_pages, rows, hd), F32),
                        pltpu.VMEM((n_pages * PAGE_SIZE // MOBA_BLOCK, MOBA_HEADS, hd), F32)],
    )
    return pl.pallas_call(
        functools.partial(_moba_sample_kernel, n_pages=n_pages, pps=pps, l_real=l_real, n_sel=n_sel),
        grid_spec=grid_spec,
        out_shape=jax.ShapeDtypeStruct((b, rows, hd), F32),
        compiler_params=_cp(("parallel", "arbitrary")),
        name="moba_sample",
    )(page_table, q, k_new, v_new, *([cache_k] * pps), *([cache_v] * pps))


def _ssd_kernel(*refs, lb, has_state):
    if has_state:
        (z_ref, x_ref, b_ref, c_ref, dt_ref, dtt_ref, arow_ref, acol_ref, cwx_ref, cwb_ref, cwc_ref,
         cbx_ref, cbb_ref, cbc_ref, dl_ref, gs_ref, px_ref, pb_ref, pc_ref, s0_ref,
         y_ref, so_ref, s_s, tx_s, tb_s, tc_s) = refs
    else:
        (z_ref, x_ref, b_ref, c_ref, dt_ref, dtt_ref, arow_ref, acol_ref, cwx_ref, cwb_ref, cwc_ref,
         cbx_ref, cbb_ref, cbc_ref, dl_ref, gs_ref,
         y_ref, so_ref, s_s, tx_s, tb_s, tc_s) = refs
    c = pl.program_id(2)
    q = SSD_Q
    kc = SSD_CONV

    @pl.when(c == 0)
    def _():
        if has_state:
            s_s[...] = s0_ref[0]
            for t_s, p_ref in ((tx_s, px_ref), (tb_s, pb_ref), (tc_s, pc_ref)):
                t_s[...] = jnp.zeros_like(t_s)
                t_s[8 - (kc - 1):8, :] = p_ref[0]
        else:
            s_s[...] = jnp.zeros_like(s_s)
            tx_s[...] = jnp.zeros_like(tx_s)
            tb_s[...] = jnp.zeros_like(tb_s)
            tc_s[...] = jnp.zeros_like(tc_s)

    def conv(x, tail_s, w_ref, bias_ref):
        prev8 = tail_s[...]
        row8 = _iota((8, x.shape[1]), 0)
        out = x * w_ref[kc - 1:kc, :]
        for j in range(1, kc):
            xs = pltpu.roll(x, j, 0)
            head = jnp.where(row8 < j, pltpu.roll(prev8, j, 0), xs[0:8])
            sh = head if lb == 8 else jnp.concatenate([head, xs[8:]], axis=0)
            out = out + sh * w_ref[kc - 1 - j:kc - j, :]
        tail_s[...] = x[lb - 8:lb]
        return _silu(out + bias_ref[...])

    xc = _pad_rows(conv(x_ref[0], tx_s, cwx_ref, cbx_ref), q)
    bc = _pad_rows(conv(b_ref[0], tb_s, cwb_ref, cbb_ref), q)
    cc = _pad_rows(conv(c_ref[0], tc_s, cwc_ref, cbc_ref), q)
    dt = dt_ref[0, 0]
    dtt = dtt_ref[0, 0]
    a_row = -jnp.exp(arow_ref[0])
    a_col = -jnp.exp(acol_ref[0])
    ri = _iota((q, q), 0)
    ci = _iota((q, q), 1)
    causal = ri >= ci
    cs = _exact_left(causal, dt * a_row)
    cst = _exact_right(dtt * a_col, ri <= ci)
    cb = _bdot_nt(cc, bc)
    lane = _iota((q, LANES), 1)
    low = lane < SSD_HEAD_DIM
    row128 = _iota((2 * SSD_HEAD_DIM, 1), 0)
    prs = range(4)
    xps = [xc[:, pp * LANES:(pp + 1) * LANES] for pp in prs]
    sps = [s_s[pp] for pp in prs]
    inter = [_bdot_nt(cc, sps[pp]) for pp in prs]
    ws = []
    for h in range(2 * len(prs)):
        dec = jnp.where(causal, jnp.exp(cs[:, h:h + 1] - cst[h:h + 1, :]), 0.0)
        ws.append(cb * dec * dtt[h:h + 1, :])
    intra = [_bdot(ws[2 * pp], jnp.where(low, xps[pp], 0.0)) + _bdot(ws[2 * pp + 1], jnp.where(low, 0.0, xps[pp]))
             for pp in prs]
    csa = [cs[:, 2 * pp:2 * pp + 1] for pp in prs]
    csb = [cs[:, 2 * pp + 1:2 * pp + 2] for pp in prs]
    tails = [jnp.where(low, jnp.exp(csa[pp][q - 1:q, :] - csa[pp]) * dt[:, 2 * pp:2 * pp + 1],
                       jnp.exp(csb[pp][q - 1:q, :] - csb[pp]) * dt[:, 2 * pp + 1:2 * pp + 2]) for pp in prs]
    upds = [_bdot_tn(xps[pp] * tails[pp], bc) for pp in prs]
    ys = []
    for pp in prs:
        s_s[pp] = sps[pp] * jnp.where(row128 < SSD_HEAD_DIM, jnp.exp(csa[pp][q - 1:q, :]),
                                      jnp.exp(csb[pp][q - 1:q, :])) + upds[pp]
        ys.append(intra[pp] + inter[pp] * jnp.where(low, jnp.exp(csa[pp]), jnp.exp(csb[pp]))
                  + dl_ref[:, pp * LANES:(pp + 1) * LANES] * xps[pp])
    y = jnp.concatenate(ys, axis=1)[0:lb]
    y = y * _silu(z_ref[0])
    y_ref[0] = _rms(y, gs_ref[...])

    @pl.when(c == pl.num_programs(2) - 1)
    def _():
        so_ref[0] = s_s[...]


def _ssd(pm, dt, dtt, conv_w, conv_b, a_log, d_skip, g_ssd, conv_prev, s0, *, lb, state_off=0):
    b, l, _ = pm.shape
    nc = l // lb
    hg = SSD_HEADS // SSD_GROUPS
    gw = SSD_INNER // SSD_GROUPS
    has_state = s0 is not None
    cwx, cwb, cwc = conv_w[:, :SSD_INNER], conv_w[:, SSD_INNER:SSD_INNER + 512], conv_w[:, SSD_INNER + 512:]
    cbx, cbb, cbc = (conv_b[None, :SSD_INNER], conv_b[None, SSD_INNER:SSD_INNER + 512],
                     conv_b[None, SSD_INNER + 512:])
    a_pad = jnp.pad(a_log.reshape(SSD_GROUPS, 1, hg), ((0, 0), (0, 0), (0, LANES - hg)))
    a_col = a_log.reshape(SSD_GROUPS, hg, 1)
    d_lane = jnp.repeat(d_skip, SSD_HEAD_DIM)[None, :]
    in_specs = [
        pl.BlockSpec((1, lb, gw), lambda bi, g, c: (bi, c, g)),
        pl.BlockSpec((1, lb, gw), lambda bi, g, c: (bi, c, 4 + g)),
        pl.BlockSpec((1, lb, LANES), lambda bi, g, c: (bi, c, 32 + g)),
        pl.BlockSpec((1, lb, LANES), lambda bi, g, c: (bi, c, 36 + g)),
        pl.BlockSpec((1, 1, SSD_Q, LANES), lambda bi, g, c: (bi, g, c, 0)),
        pl.BlockSpec((1, 1, hg, SSD_Q), lambda bi, g, c: (bi, g, 0, c)),
        pl.BlockSpec((1, 1, LANES), lambda bi, g, c: (g, 0, 0)),
        pl.BlockSpec((1, hg, 1), lambda bi, g, c: (g, 0, 0)),
        pl.BlockSpec((SSD_CONV, gw), lambda bi, g, c: (0, g)),
        pl.BlockSpec((SSD_CONV, LANES), lambda bi, g, c: (0, g)),
        pl.BlockSpec((SSD_CONV, LANES), lambda bi, g, c: (0, g)),
        pl.BlockSpec((1, gw), lambda bi, g, c: (0, g)),
        pl.BlockSpec((1, LANES), lambda bi, g, c: (0, g)),
        pl.BlockSpec((1, LANES), lambda bi, g, c: (0, g)),
        pl.BlockSpec((1, gw), lambda bi, g, c: (0, g)),
        pl.BlockSpec((1, gw), lambda bi, g, c: (0, g)),
    ]
    args = [pm, pm, pm, pm, dt, dtt, a_pad, a_col, cwx, cwb, cwc, cbx, cbb, cbc, d_lane, g_ssd[None, :]]
    if has_state:
        in_specs += [
            pl.BlockSpec((1, SSD_CONV - 1, gw), lambda bi, g, c: (state_off + bi, 0, g)),
            pl.BlockSpec((1, SSD_CONV - 1, LANES), lambda bi, g, c: (state_off + bi, 0, 16 + g)),
            pl.BlockSpec((1, SSD_CONV - 1, LANES), lambda bi, g, c: (state_off + bi, 0, 20 + g)),
            pl.BlockSpec((1, 4, LANES, LANES), lambda bi, g, c: (state_off + bi, g, 0, 0)),
        ]
        args += [conv_prev, conv_prev, conv_prev, s0]
    y, s_out = pl.pallas_call(
        functools.partial(_ssd_kernel, lb=lb, has_state=has_state),
        grid=(b, SSD_GROUPS, nc),
        in_specs=in_specs,
        out_specs=[pl.BlockSpec((1, lb, gw), lambda bi, g, c: (bi, c, g)),
                   pl.BlockSpec((1, 4, LANES, LANES), lambda bi, g, c: (bi, g, 0, 0))],
        out_shape=[jax.ShapeDtypeStruct((b, l, SSD_INNER), F32),
                   jax.ShapeDtypeStruct((b, SSD_HEADS // 2, LANES, LANES), F32)],
        scratch_shapes=[pltpu.VMEM((4, LANES, LANES), F32), pltpu.VMEM((8, gw), F32),
                        pltpu.VMEM((8, LANES), F32), pltpu.VMEM((8, LANES), F32)],
        compiler_params=_cp(("parallel", "parallel", "arbitrary")),
        name="ssd",
    )(*args)
    return y, s_out


def _dt_kernel(x_ref, g_ref, w_ref, wt_ref, bias_ref, biast_ref, o_ref, ot_ref):
    a = _rms(x_ref[...], g_ref[...]).astype(BF16)
    o_ref[...] = _softplus(jnp.dot(a, w_ref[...], preferred_element_type=F32) + bias_ref[...])
    t = lax.dot_general(wt_ref[...], a, (((1,), (1,)), ((), ())), preferred_element_type=F32)
    ot_ref[...] = _softplus(t + biast_ref[...])


def _dt_proj(x, gain, w_dt, dt_bias, *, tm=1024):
    m = x.shape[0]
    tm = min(tm, m)
    w_pad = jnp.pad(w_dt, ((0, 0), (0, LANES - SSD_HEADS))).astype(BF16)
    wt = w_dt.T.astype(BF16)
    bias_pad = jnp.pad(dt_bias, (0, LANES - SSD_HEADS))[None, :]
    return pl.pallas_call(
        _dt_kernel,
        grid=(m // tm,),
        in_specs=[pl.BlockSpec((tm, D_MODEL), lambda i: (i, 0)),
                  pl.BlockSpec((1, D_MODEL), lambda i: (0, 0)),
                  pl.BlockSpec((D_MODEL, LANES), lambda i: (0, 0)),
                  pl.BlockSpec((SSD_HEADS, D_MODEL), lambda i: (0, 0)),
                  pl.BlockSpec((1, LANES), lambda i: (0, 0)),
                  pl.BlockSpec((SSD_HEADS, 1), lambda i: (0, 0))],
        out_specs=[pl.BlockSpec((tm, LANES), lambda i: (i, 0)),
                   pl.BlockSpec((SSD_HEADS, tm), lambda i: (0, i))],
        out_shape=[jax.ShapeDtypeStruct((m, LANES), F32), jax.ShapeDtypeStruct((SSD_HEADS, m), F32)],
        compiler_params=_cp(("parallel",)),
        name="dt_proj",
    )(x, gain.reshape(1, -1), w_pad, wt, bias_pad, dt_bias[:, None])


def _gla_levels(n_valid):
    s, levels = 1, []
    while s < n_valid:
        levels.append(s)
        s *= 2
    return levels[::-1]


def _gla_kernel(*refs, lb, n_valid, has_state):
    if has_state:
        (v_ref, gg_ref, q_ref, k_ref, gf_ref, wf_ref, bf_ref, gn_ref, s0_ref, o_ref, so_ref, st_s) = refs
    else:
        (v_ref, gg_ref, q_ref, k_ref, gf_ref, wf_ref, bf_ref, gn_ref, o_ref, so_ref, st_s) = refs
    c = pl.program_id(1)
    qn = GLA_Q
    heads = range(GLA_HEADS)
    dk, dv = GLA_DK, GLA_DV

    @pl.when(c == 0)
    def _():
        for h in heads:
            if has_state:
                st_s[h] = s0_ref[0, h].T
            else:
                st_s[h] = jnp.zeros((dv, dk), F32)

    logits = _bdot(_pad_rows(gf_ref[0], qn), wf_ref[...]) + bf_ref[...]
    logf = -_softplus(-logits) * (1.0 / GLA_GATE_NORM)
    logf = jnp.where(_iota(logf.shape, 0) < n_valid, logf, 0.0)

    ri = _iota((qn, qn), 0)
    ci = _iota((qn, qn), 1)
    levels = _gla_levels(min(n_valid, qn))
    mats = [ri >= ci]
    masks = []
    for s in levels:
        same = _div(ri, s) == _div(ci, s)
        mats.append(same & (ri >= ci))
        mats.append(same)
        masks.append((_div(ri, 2 * s) == _div(ci, 2 * s)) & (_mod(_div(ri, s), 2) == 1)
                     & (_mod(_div(ci, s), 2) == 0))
    stack = jnp.concatenate([mm.astype(BF16) for mm in mats], axis=0)
    cums = _exact_left(stack, logf)

    qs = [_pad_rows(q_ref[0, :, h * dk:(h + 1) * dk], qn) * (dk ** -0.5) for h in heads]
    ks = [_pad_rows(k_ref[0, :, h * dk:(h + 1) * dk], qn) for h in heads]
    vs = [_pad_rows(v_ref[0, :, h * dv:(h + 1) * dv], qn) for h in heads]
    cbs = [cums[0:qn, h * dk:(h + 1) * dk] for h in heads]
    sts = [st_s[h] for h in heads]
    inter = [_bdot_nt(qs[h] * jnp.exp(cbs[h]), sts[h]) for h in heads]
    atts = [jnp.where(ri == ci, jnp.sum(qs[h] * ks[h], axis=-1, keepdims=True), 0.0) for h in heads]
    for idx in range(len(levels)):
        prods = []
        for h in heads:
            lq = cums[(1 + 2 * idx) * qn:(2 + 2 * idx) * qn, h * dk:(h + 1) * dk]
            bs = cums[(2 + 2 * idx) * qn:(3 + 2 * idx) * qn, h * dk:(h + 1) * dk]
            prods.append(_bdot_nt(qs[h] * jnp.exp(lq), ks[h] * jnp.exp(bs - lq)))
        atts = [atts[h] + jnp.where(masks[idx], prods[h], 0.0) for h in heads]
    os_ = [_bdot(atts[h], vs[h]) + inter[h] for h in heads]
    cbls = [cbs[h][qn - 1:qn, :] for h in heads]
    upds = [_bdot_tn(vs[h], ks[h] * jnp.exp(cbls[h] - cbs[h])) for h in heads]
    for h in heads:
        st_s[h] = sts[h] * jnp.exp(cbls[h]) + upds[h]
        o_ref[0, :, h * dv:(h + 1) * dv] = (_rms(os_[h][0:lb], gn_ref[...])
                                            * _silu(gg_ref[0, :, h * dv:(h + 1) * dv]))

    @pl.when(c == pl.num_programs(1) - 1)
    def _():
        for h in heads:
            so_ref[0, h] = st_s[h].T


def _gla(pg, gf, w_f2, b_f, g_gla, s0, *, lb, n_valid, state_off=0):
    b, l, _ = pg.shape
    nc = l // lb
    has_state = s0 is not None
    wf_pad = jnp.pad(w_f2, ((0, LANES - GLA_GATE_RANK), (0, 0))).astype(BF16)
    hv = GLA_HEADS * GLA_DV
    hk = GLA_HEADS * GLA_DK
    st_spec = pl.BlockSpec((1, GLA_HEADS, GLA_DK, GLA_DV), lambda bi, c: (bi, 0, 0, 0))
    in_specs = [
        pl.BlockSpec((1, lb, hv), lambda bi, c: (bi, c, 0)),
        pl.BlockSpec((1, lb, hv), lambda bi, c: (bi, c, 1)),
        pl.BlockSpec((1, lb, hk), lambda bi, c: (bi, c, 4)),
        pl.BlockSpec((1, lb, hk), lambda bi, c: (bi, c, 5)),
        pl.BlockSpec((1, lb, LANES), lambda bi, c: (bi, c, 0)),
        pl.BlockSpec((LANES, hk), lambda bi, c: (0, 0)),
        pl.BlockSpec((1, hk), lambda bi, c: (0, 0)),
        pl.BlockSpec((1, GLA_DV), lambda bi, c: (0, 0)),
    ]
    args = [pg, pg, pg, pg, gf, wf_pad, b_f[None, :], g_gla[None, :]]
    if has_state:
        in_specs.append(pl.BlockSpec((1, GLA_HEADS, GLA_DK, GLA_DV), lambda bi, c: (state_off + bi, 0, 0, 0)))
        args.append(s0)
    return pl.pallas_call(
        functools.partial(_gla_kernel, lb=lb, n_valid=n_valid, has_state=has_state),
        grid=(b, nc),
        in_specs=in_specs,
        out_specs=[pl.BlockSpec((1, lb, hv), lambda bi, c: (bi, c, 0)), st_spec],
        out_shape=[jax.ShapeDtypeStruct((b, l, hv), F32),
                   jax.ShapeDtypeStruct((b, GLA_HEADS, GLA_DK, GLA_DV), F32)],
        scratch_shapes=[pltpu.VMEM((GLA_HEADS, GLA_DV, GLA_DK), F32)],
        compiler_params=_cp(("parallel", "arbitrary")),
        name="gla",
    )(*args)


def _rwkv_prep_kernel(*refs, tiles_per_seq, prev_given):
    if prev_given:
        (cur_ref, prev_ref, mu_ref, w12_ref, wg2_ref, w0_ref, a0_ref, kk_ref, ka_ref,
         r_o, lw_o, k_o, v_o, kk_o, a_o, g_o) = refs
    else:
        (cur_ref, mu_ref, w12_ref, wg2_ref, w0_ref, a0_ref, kk_ref, ka_ref,
         r_o, lw_o, k_o, v_o, kk_o, a_o, g_o, last_s) = refs
    d = D_MODEL
    cur = cur_ref[...]
    if prev_given:
        prev = prev_ref[...]
    else:
        i = pl.program_id(0)

        @pl.when(i % tiles_per_seq == 0)
        def _():
            last_s[...] = jnp.zeros_like(last_s)

        row = _iota(cur.shape, 0)
        first = jnp.broadcast_to(last_s[7:8, :], cur.shape)
        prev = jnp.where(row == 0, first, pltpu.roll(cur, 1, 0))
        last_s[...] = cur[cur.shape[0] - 8:]
    mixed = cur + (prev - cur) * mu_ref[...]
    r = mixed[:, 0:d]
    kc = mixed[:, d:2 * d]
    vc = mixed[:, 2 * d:3 * d]
    t1 = mixed[:, 3 * d:3 * d + LANES]
    lane = _iota(t1.shape, 1)
    t1 = jnp.where(lane < 64, jnp.tanh(t1), t1)
    lora = _bdot(t1, w12_ref[...])
    logw = -_softplus(-(w0_ref[...] + lora[:, 0:d])) - 0.5
    a = _sigmoid(a0_ref[...] + lora[:, d:2 * d])
    g = _bdot(_sigmoid(mixed[:, 3 * d + LANES:3 * d + 2 * LANES]), wg2_ref[...])
    kk = kc * kk_ref[...]
    ss = _seg_sum(kk * kk, _seg_ones(RWKV_HEAD_DIM))
    kk = kk * lax.rsqrt(jnp.maximum(ss, 1e-24))
    r_o[...] = r
    lw_o[...] = -jnp.exp(logw)
    k_o[...] = kc * (1.0 + (a - 1.0) * ka_ref[...])
    v_o[...] = vc
    kk_o[...] = kk
    a_o[...] = a
    g_o[...] = g


def _rwkv_prep(cur, prev, mu, w12, wg2, w0, a0, k_k, k_a, *, tm, seq_len):
    t = cur.shape[0]
    tm = min(tm, t)
    prev_given = prev is not None
    row_spec = pl.BlockSpec((tm, RWKV_SHIFT_DIM), lambda i: (i, 0))
    vec = lambda n: pl.BlockSpec((1, n), lambda i: (0, 0))
    in_specs = [row_spec] + ([row_spec] if prev_given else []) + [
        vec(RWKV_SHIFT_DIM),
        pl.BlockSpec((LANES, 2 * D_MODEL), lambda i: (0, 0)),
        pl.BlockSpec((LANES, D_MODEL), lambda i: (0, 0)),
        vec(D_MODEL), vec(D_MODEL), vec(D_MODEL), vec(D_MODEL)]
    args = [cur] + ([prev] if prev_given else []) + [mu[None, :], w12, wg2, w0[None, :], a0[None, :],
                                                    k_k[None, :], k_a[None, :]]
    out_spec = pl.BlockSpec((tm, D_MODEL), lambda i: (i, 0))
    return pl.pallas_call(
        functools.partial(_rwkv_prep_kernel, tiles_per_seq=max(seq_len // tm, 1), prev_given=prev_given),
        grid=(t // tm,),
        in_specs=in_specs,
        out_specs=[out_spec] * 7,
        out_shape=[jax.ShapeDtypeStruct((t, D_MODEL), F32)] * 7,
        scratch_shapes=[] if prev_given else [pltpu.VMEM((8, RWKV_SHIFT_DIM), F32)],
        compiler_params=_cp(("arbitrary",)),
        name="rwkv_prep",
    )(*args)


def _rwkv_kernel(*refs, lb, has_state):
    if has_state:
        (r_ref, lw_ref, k_ref, v_ref, kk_ref, a_ref, g_ref, rk_ref, gw_ref, gb_ref, s0_ref,
         y_ref, so_ref, s_s) = refs
    else:
        (r_ref, lw_ref, k_ref, v_ref, kk_ref, a_ref, g_ref, rk_ref, gw_ref, gb_ref,
         y_ref, so_ref, s_s) = refs
    c = pl.program_id(1)
    qn = RWKV_Q
    hd = RWKV_HEAD_DIM

    @pl.when(c == 0)
    def _():
        if has_state:
            s_s[...] = s0_ref[0]
        else:
            s_s[...] = jnp.zeros_like(s_s)

    ri = _iota((qn, qn), 0)
    ci = _iota((qn, qn), 1)
    lw_all = _pad_rows(lw_ref[0], qn)
    cw_all = _exact_left(ri >= ci, lw_all)
    lane = _iota((qn, LANES), 1)
    low = lane < hd
    rowq = _iota((qn, LANES), 0)
    lane_m = _mod(lane, hd)
    strict = rowq > lane_m
    incl = rowq >= lane_m
    r2 = _iota((2 * qn, LANES), 0)
    l2 = _iota((2 * qn, LANES), 1)
    bd = (r2 < qn) == (l2 < hd)
    strict_bd = bd & (_mod(r2, qn) > _mod(l2, hd))
    eye = (r2 == l2).astype(F32)
    steps = int(math.log2(qn))
    pairs = range(RWKV_HEADS // 2)
    sls = [slice(p * LANES, (p + 1) * LANES) for p in pairs]
    split = lambda x: jnp.concatenate([jnp.where(low, x, 0.0), jnp.where(low, 0.0, x)], axis=0)
    rs, ks, vs, bbs, ats, rts, cwls, eqs, lsts, rsts = ([] for _ in range(10))
    for sl in sls:
        lw = lw_all[:, sl]
        cw = cw_all[:, sl]
        r = _pad_rows(r_ref[0, :, sl], qn)
        k = _pad_rows(k_ref[0, :, sl], qn)
        v = _pad_rows(v_ref[0, :, sl], qn)
        kk = _pad_rows(kk_ref[0, :, sl], qn)
        av = _pad_rows(a_ref[0, :, sl], qn)
        cwl = cw[qn - 1:qn, :]
        e_neg = jnp.exp(-cw)
        bb = kk * av
        at = -kk * jnp.exp(cw - lw)
        rt = r * jnp.exp(cw)
        rs.append(r)
        ks.append(k)
        vs.append(v)
        bbs.append(bb)
        ats.append(at)
        rts.append(rt)
        cwls.append(cwl)
        eqs.append(jnp.exp(cwl - cw))
        lsts.append(jnp.concatenate([split(at), split(rt)], axis=0))
        rsts.append(jnp.concatenate([bb * e_neg, k * e_neg], axis=0))
    pms = [_bdot_nt(lsts[p], rsts[p]) for p in pairs]
    s_bds = [s_s[p] for p in pairs]
    ars = [_bdot_nt(jnp.concatenate([ats[p], rts[p]], axis=0), s_bds[p]) for p in pairs]
    n_bds, aaks, arks = [], [], []
    for pm in pms:
        p0, p1, p2, p3 = pm[0:qn], pm[qn:2 * qn], pm[2 * qn:3 * qn], pm[3 * qn:4 * qn]
        n_bds.append(jnp.where(strict_bd, jnp.concatenate([p0, pltpu.roll(p1, hd, 1)], axis=0), 0.0))
        aaks.append(jnp.where(strict, jnp.where(low, pltpu.roll(p0, hd, 1), p1), 0.0))
        arb = jnp.where(incl, jnp.where(low, p2, pltpu.roll(p3, hd, 1)), 0.0)
        ark = jnp.where(incl, jnp.where(low, pltpu.roll(p2, hd, 1), p3), 0.0)
        arks.append(jnp.concatenate([arb, ark], axis=1))
    mpows = [_bdot(n, n) for n in n_bds]
    t_invs = [eye + n for n in n_bds]
    for _ in range(steps - 2):
        prods = [_bdot(jnp.concatenate([t_invs[p], mpows[p]], axis=0), mpows[p]) for p in pairs]
        t_invs = [t_invs[p] + prods[p][0:2 * qn] for p in pairs]
        mpows = [prods[p][2 * qn:4 * qn] for p in pairs]
    t_invs = [t_invs[p] + _bdot(t_invs[p], mpows[p]) for p in pairs]
    v_ms = [split(v) for v in vs]
    rhss = [ars[p][0:qn] + _bdot(aaks[p], v_ms[p]) for p in pairs]
    u2s = [_bdot(t_invs[p], split(rhss[p])) for p in pairs]
    us = [u2[0:qn] + u2[qn:2 * qn] for u2 in u2s]
    ys = [ars[p][qn:2 * qn] + _bdot(arks[p], jnp.concatenate([split(us[p]), v_ms[p]], axis=0)) for p in pairs]
    upds = [_bdot_tn(jnp.concatenate([us[p], vs[p]], axis=0),
                     jnp.concatenate([bbs[p] * eqs[p], ks[p] * eqs[p]], axis=0)) for p in pairs]
    for p in pairs:
        s_s[p] = s_bds[p] * jnp.exp(cwls[p]) + jnp.where(bd, upds[p], 0.0)
    inv_n = 1.0 / hd
    lowb = low[0:lb]

    def half_sum(x):
        lo = jnp.sum(jnp.where(lowb, x, 0.0), axis=-1, keepdims=True)
        hi = jnp.sum(jnp.where(lowb, 0.0, x), axis=-1, keepdims=True)
        return jnp.where(lowb, lo, hi)

    for p, sl in enumerate(sls):
        y = ys[p][0:lb]
        mu = half_sum(y) * inv_n
        yc = y - mu
        var = half_sum(yc * yc) * inv_n
        yn = yc * lax.rsqrt(var + RWKV_GN_EPS) * gw_ref[:, sl] + gb_ref[:, sl]
        bonus = half_sum(rs[p][0:lb] * ks[p][0:lb] * rk_ref[:, sl]) * vs[p][0:lb]
        y_ref[0, :, sl] = (yn + bonus) * g_ref[0, :, sl]

    @pl.when(c == pl.num_programs(1) - 1)
    def _():
        so_ref[0] = s_s[...]


def _rwkv(r, lw, k, v, kk, a, g, r_k, gn_w, gn_b, s0_bd, *, lb):
    b, l, _ = r.shape
    nc = l // lb
    has_state = s0_bd is not None
    tok = pl.BlockSpec((1, lb, D_MODEL), lambda bi, c: (bi, c, 0))
    vec = pl.BlockSpec((1, D_MODEL), lambda bi, c: (0, 0))
    st = pl.BlockSpec((1, RWKV_HEADS // 2, LANES, LANES), lambda bi, c: (bi, 0, 0, 0))
    in_specs = [tok] * 7 + [vec] * 3 + ([st] if has_state else [])
    args = [r, lw, k, v, kk, a, g, r_k.reshape(1, -1), gn_w[None, :], gn_b[None, :]] + ([s0_bd] if has_state else [])
    return pl.pallas_call(
        functools.partial(_rwkv_kernel, lb=lb, has_state=has_state),
        grid=(b, nc),
        in_specs=in_specs,
        out_specs=[tok, st],
        out_shape=[jax.ShapeDtypeStruct((b, l, D_MODEL), F32),
                   jax.ShapeDtypeStruct((b, RWKV_HEADS // 2, LANES, LANES), F32)],
        scratch_shapes=[pltpu.VMEM((RWKV_HEADS // 2, LANES, LANES), F32)],
        compiler_params=_cp(("parallel", "arbitrary")),
        name="rwkv",
    )(*args)


def _memattn_kernel(q_ref, mk_ref, mv_ref, o_ref):
    scale = MEM_HEAD_DIM ** -0.5
    for h in range(MEM_HEADS):
        sl = slice(h * MEM_HEAD_DIM, (h + 1) * MEM_HEAD_DIM)
        s = _bdot_nt(q_ref[0, :, sl], mk_ref[0, 0, :, sl]) * scale
        m = jnp.max(s, axis=-1, keepdims=True)
        p = jnp.exp(s - m)
        l = jnp.sum(p, axis=-1, keepdims=True)
        o_ref[0, :, sl] = _bdot(p / l, mv_ref[0, 0, :, sl])


def _mem_attn(q, mk, mv, layer, *, tq):
    b, l, _ = q.shape
    tq = min(tq, l)
    mem_spec = pl.BlockSpec((1, 1, N_MEM, D_MODEL), lambda bi, i: (layer, bi, 0, 0))
    return pl.pallas_call(
        _memattn_kernel,
        grid=(b, l // tq),
        in_specs=[pl.BlockSpec((1, tq, D_MODEL), lambda bi, i: (bi, i, 0)), mem_spec, mem_spec],
        out_specs=pl.BlockSpec((1, tq, D_MODEL), lambda bi, i: (bi, i, 0)),
        out_shape=jax.ShapeDtypeStruct((b, l, D_MODEL), F32),
        compiler_params=_cp(("parallel", "arbitrary")),
        name="mem_attn",
    )(q, mk, mv)


def _pad_tokens(x, lp):
    b, l = x.shape[0], x.shape[1]
    if l == lp:
        return x
    return jnp.pad(x, ((0, 0), (0, lp - l)) + ((0, 0),) * (x.ndim - 2))


def _even_layer(x, w, e, grp):
    b, l = grp["b"], grp["l"]
    sample = grp["sample"]
    gain = w["norm_mix"][2 * e]
    pm = _mm([x], [w["e_w_main"][e]], gain=gain, name="e_in_main")
    v = _mm([x], [w["e_w_v"][e]], gain=gain, name="e_in_v")
    dt, dtt = _dt_proj(x, gain, w["e_w_dt"][e], w["e_dt_bias"][e])
    hg = SSD_HEADS // SSD_GROUPS
    lp = l if not sample else 8
    lq = max(lp, SSD_Q)
    dt4 = dt[:, :SSD_HEADS].reshape(b, l, SSD_GROUPS, hg).transpose(0, 2, 1, 3)
    dt4 = jnp.pad(dt4, ((0, 0), (0, 0), (0, lq - l), (0, LANES - hg)))
    dtt4 = dtt.reshape(SSD_GROUPS, hg, b, l).transpose(2, 0, 1, 3)
    dtt4 = jnp.pad(dtt4, ((0, 0), (0, 0), (0, 0), (0, lq - l)))
    pm_seq = pm.reshape(b, l, -1)
    pm3 = _pad_tokens(pm_seq, lp)
    kc1 = SSD_CONV - 1
    xbc_tail = pm_seq[:, max(l - kc1, 0):, SSD_INNER:SSD_INNER + XBC_DIM]
    if sample:
        conv_prev = grp["conv"].reshape(N_EVEN * b, kc1, XBC_DIM)
        s0 = grp["ssd"].reshape(N_EVEN * b, SSD_HEADS // 2, LANES, LANES)
        conv_new = jnp.concatenate([grp["conv"][e], xbc_tail], axis=1)[:, -kc1:]
    else:
        conv_prev, s0 = None, None
        assert l >= kc1
        conv_new = xbc_tail
    y_ssd, s_new = _ssd(pm3, dt4, dtt4, w["e_conv_w"][e], w["e_conv_b"][e], w["e_a_log"][e], w["e_d_skip"][e],
                        w["e_g_ssd"][e], conv_prev, s0, lb=min(lp, SSD_Q), state_off=e * b)
    y_ssd = y_ssd[:, :l].reshape(b * l, SSD_INNER)
    s_new = s_new.reshape(b, SSD_HEADS, SSD_HEAD_DIM, SSD_STATE)
    q0 = grp["q0"]
    cos_t, sin_t = _rope_tables(q0 + jnp.arange(l))
    if sample:
        cos_t, sin_t = jnp.tile(cos_t, (b, 1)), jnp.tile(sin_t, (b, 1))
        q_r, k_r = _qk_prep(pm, 5, 6, cos_t, sin_t, w["e_g_q"][e], w["e_g_k"][e], tq=b * l, with_kmean=False)
        th = lambda t: t.reshape(b, l * MOBA_HEADS, MOBA_HEAD_DIM)
        o = _moba_sample(grp["page_table"], th(q_r), th(k_r), th(v), grp["cache_k"], grp["cache_v"],
                         layer_off=e * grp["n_pool"]).reshape(b * l, D_MODEL)
    else:
        q_r, k_r, kmean = _qk_prep(pm, 5, 6, cos_t, sin_t, w["e_g_q"][e], w["e_g_k"][e], tq=MOBA_BLOCK,
                                   with_kmean=True)
        o = _moba_prompt(q_r, k_r, v, kmean.reshape(b, l // MOBA_BLOCK, D_MODEL), b=b, l=l)
    x = _mm([y_ssd, o], [w["e_w_out_a"][e], w["e_w_out_b"][e]], res=x, tm=512, name="e_out")
    hd = (b, l, MOBA_HEADS, MOBA_HEAD_DIM)
    return x, s_new, conv_new, k_r.reshape(hd), v.reshape(hd)


def _odd_layer(x, w, o, grp):
    b, l = grp["b"], grp["l"]
    sample = grp["sample"]
    gain = w["norm_mix"][2 * o + 1]
    cur = _mm([x], [w["o_w_cur"][o]], gain=gain, tn=1664, name="o_in_cur")
    pg = _mm([x], [w["o_w_g"][o]], gain=gain, name="o_in_g")
    gf = _mm([x], [w["o_w_gf"][o]], gain=gain, name="o_in_gf")
    cur3 = cur.reshape(b, l, RWKV_SHIFT_DIM)
    if sample:
        prev = jnp.concatenate([grp["shift"][o][:, None], cur3[:, :-1]], axis=1).reshape(b * l, RWKV_SHIFT_DIM)
    else:
        prev = None
    vecs = _rwkv_prep(cur, prev, w["o_mu"][o], w["o_w12"][o], w["o_w_g2"][o], w["o_w0"][o], w["o_a0"][o],
                      w["o_k_k"][o], w["o_k_a"][o], tm=512, seq_len=l)
    lp = l if not sample else 8
    vecs3 = [_pad_tokens(t.reshape(b, l, D_MODEL), lp) for t in vecs]
    if sample:
        s0 = grp["rwkv"][o].reshape(b, RWKV_HEADS // 2, 2, RWKV_HEAD_DIM, RWKV_HEAD_DIM)
        z = jnp.zeros_like(s0[:, :, 0])
        s0_bd = jnp.concatenate([jnp.concatenate([s0[:, :, 0], z], axis=-1),
                                 jnp.concatenate([z, s0[:, :, 1]], axis=-1)], axis=-2)
    else:
        s0_bd = None
    y_r, s_bd = _rwkv(*vecs3, w["o_r_k"][o], w["o_gn_w"][o], w["o_gn_b"][o], s0_bd, lb=min(lp, RWKV_Q))
    y_r = y_r[:, :l].reshape(b * l, D_MODEL)
    hd = RWKV_HEAD_DIM
    s_r = jnp.stack([s_bd[:, :, :hd, :hd], s_bd[:, :, hd:, hd:]], axis=2).reshape(b, RWKV_HEADS, hd, hd)
    pg3 = _pad_tokens(pg.reshape(b, l, -1), lp)
    gf3 = _pad_tokens(gf.reshape(b, l, LANES), lp)
    s0_g = grp["gla"].reshape(N_ODD * b, GLA_HEADS, GLA_DK, GLA_DV) if sample else None
    og, s_g = _gla(pg3, gf3, w["o_w_f2"][o], w["o_b_f"][o], w["o_g_gla"][o], s0_g,
                   lb=min(lp, GLA_Q), n_valid=min(l, GLA_Q), state_off=o * b)
    og = og[:, :l].reshape(b * l, GLA_HEADS * GLA_DV)
    x = _mm([y_r, og], [w["o_w_out_a"][o], w["o_w_out_b"][o]], res=x, tm=512, name="o_out")
    return x, s_r, cur3[:, -1], s_g


def _trunk(x, w, grp):
    b, l = grp["b"], grp["l"]
    outs = {k: [] for k in ("ssd", "conv", "k", "v", "rwkv", "shift", "gla")}
    for layer in range(DEPTH):
        if layer % 2 == 0:
            x, s, cnew, k, v = _even_layer(x, w, layer // 2, grp)
            outs["ssd"].append(s)
            outs["conv"].append(cnew)
            outs["k"].append(k)
            outs["v"].append(v)
        else:
            x, s_r, sh, s_g = _odd_layer(x, w, layer // 2, grp)
            outs["rwkv"].append(s_r)
            outs["shift"].append(sh)
            outs["gla"].append(s_g)
        qm = _mm([x], [w["m_w_q"][layer]], gain=w["norm_mem"][layer], head_gain=w["m_g_q"][layer], name="mem_q")
        lp = 8 if grp["sample"] else l
        om = _mem_attn(_pad_tokens(qm.reshape(b, l, D_MODEL), lp), grp["mem_k"], grp["mem_v"], layer, tq=512)
        x = _mm([om[:, :l].reshape(b * l, D_MODEL)], [w["m_w_o"][layer]], res=x, name="mem_o")
        x = _ffn(x, w["norm_ffn"][layer], w["f_w_g"][layer], w["f_w_u"][layer], w["f_w_d"][layer])
    return x, {k: jnp.stack(v) for k, v in outs.items()}


def kernel(x_prompt, x_sample, cache_moba_k, cache_moba_v, state_ssd, state_ssd_conv, state_rwkv, state_rwkv_shift, state_gla, cache_mem_k, cache_mem_v, page_table, mem_prompt, norm_mix, norm_mem, norm_memtok, norm_ffn, e_w_in, e_conv_w, e_conv_b, e_dt_bias, e_a_log, e_d_skip, e_g_ssd, e_g_q, e_g_k, e_w_out, o_w_in, o_mu, o_w0, o_w_w2, o_a0, o_w_a2, o_w_g2, o_k_k, o_k_a, o_r_k, o_gn_w, o_gn_b, o_w_f2, o_b_f, o_g_gla, o_w_out, m_w_q, m_w_kv, m_g_q, m_g_k, m_w_o, f_w_gu, f_w_down):
    bp, lp, _ = x_prompt.shape
    bs, ls, _ = x_sample.shape
    n_pool = cache_moba_k.shape[1]
    n_pages = page_table.shape[1]
    c_dt = SSD_INNER + XBC_DIM
    c_q = c_dt + SSD_HEADS
    c_cur = RWKV_SHIFT_DIM
    z64 = jnp.zeros((N_ODD, 64, D_MODEL), F32)
    w = dict(
        norm_mix=norm_mix, norm_mem=norm_mem, norm_ffn=norm_ffn,
        e_w_main=jnp.concatenate([e_w_in[:, :, :c_dt], e_w_in[:, :, c_q:c_q + 2 * D_MODEL]], axis=-1).astype(BF16),
        e_w_v=e_w_in[:, :, c_q + 2 * D_MODEL:].astype(BF16),
        e_w_dt=e_w_in[:, :, c_dt:c_q],
        e_conv_w=e_conv_w, e_conv_b=e_conv_b, e_dt_bias=e_dt_bias, e_a_log=e_a_log, e_d_skip=e_d_skip,
        e_g_ssd=e_g_ssd, e_g_q=e_g_q, e_g_k=e_g_k,
        e_w_out_a=e_w_out[:, :SSD_INNER].astype(BF16), e_w_out_b=e_w_out[:, SSD_INNER:].astype(BF16),
        o_w_cur=o_w_in[:, :, :c_cur].astype(BF16),
        o_w_g=jnp.concatenate([o_w_in[:, :, c_cur + 2 * GLA_KEY_DIM:c_cur + 2 * GLA_KEY_DIM + 2 * D_MODEL],
                               o_w_in[:, :, c_cur:c_cur + 2 * GLA_KEY_DIM]], axis=-1).astype(BF16),
        o_w_gf=jnp.pad(o_w_in[:, :, c_cur + 2 * GLA_KEY_DIM + 2 * D_MODEL:],
                       ((0, 0), (0, 0), (0, LANES - GLA_GATE_RANK))).astype(BF16),
        o_mu=o_mu, o_w0=o_w0, o_a0=o_a0, o_k_k=o_k_k, o_k_a=o_k_a, o_r_k=o_r_k, o_gn_w=o_gn_w, o_gn_b=o_gn_b,
        o_w12=jnp.concatenate([jnp.concatenate([o_w_w2, z64], axis=-1),
                               jnp.concatenate([z64, o_w_a2], axis=-1)], axis=1).astype(BF16),
        o_w_g2=o_w_g2.astype(BF16), o_w_f2=o_w_f2, o_b_f=o_b_f, o_g_gla=o_g_gla,
        o_w_out_a=o_w_out[:, :D_MODEL].astype(BF16), o_w_out_b=o_w_out[:, D_MODEL:].astype(BF16),
        m_w_q=m_w_q.astype(BF16), m_g_q=m_g_q, m_w_o=m_w_o.astype(BF16),
        f_w_g=f_w_gu[:, :, :D_FF].astype(BF16), f_w_u=f_w_gu[:, :, D_FF:].astype(BF16),
        f_w_d=f_w_down.astype(BF16),
    )
    mem2 = mem_prompt.reshape(bp * N_MEM, D_MODEL)
    mks, mvs = [], []
    for layer in range(DEPTH):
        wkv = m_w_kv[layer].astype(BF16)
        mks.append(_mm([mem2], [wkv[:, :D_MODEL]], gain=norm_memtok[layer], head_gain=m_g_k[layer], name="mem_k"))
        mvs.append(_mm([mem2], [wkv[:, D_MODEL:]], gain=norm_memtok[layer], name="mem_v"))
    mem_k_p = jnp.stack(mks).reshape(DEPTH, bp, N_MEM, D_MODEL)
    mem_v_p = jnp.stack(mvs).reshape(DEPTH, bp, N_MEM, D_MODEL)

    grp_p = dict(b=bp, l=lp, sample=False, q0=0, mem_k=mem_k_p, mem_v=mem_v_p)
    y_p, o_p = _trunk(x_prompt.reshape(bp * lp, D_MODEL), w, grp_p)

    grp_s = dict(b=bs, l=ls, sample=True, q0=n_pages * PAGE_SIZE, page_table=page_table, n_pool=n_pool,
                 cache_k=cache_moba_k.reshape(N_EVEN * n_pool, PAGE_SIZE * MOBA_HEADS, MOBA_HEAD_DIM),
                 cache_v=cache_moba_v.reshape(N_EVEN * n_pool, PAGE_SIZE * MOBA_HEADS, MOBA_HEAD_DIM),
                 ssd=state_ssd, conv=state_ssd_conv, rwkv=state_rwkv, shift=state_rwkv_shift, gla=state_gla,
                 mem_k=cache_mem_k.reshape(DEPTH, bs, N_MEM, D_MODEL),
                 mem_v=cache_mem_v.reshape(DEPTH, bs, N_MEM, D_MODEL))
    y_s, o_s = _trunk(x_sample.reshape(bs * ls, D_MODEL), w, grp_s)

    mem_shape = (DEPTH, bp, N_MEM, MEM_HEADS, MEM_HEAD_DIM)
    return (y_p.reshape(bp, lp, D_MODEL), y_s.reshape(bs, ls, D_MODEL),
            o_p["k"], o_p["v"], o_s["k"], o_s["v"], o_p["ssd"], o_s["ssd"], o_p["conv"], o_s["conv"],
            o_p["rwkv"], o_s["rwkv"], o_p["shift"], o_s["shift"], o_p["gla"], o_s["gla"],
            mem_k_p.reshape(mem_shape), mem_v_p.reshape(mem_shape))
```

```python
import functools
import math

import jax
import jax.numpy as jnp
from jax import lax
from jax.experimental import pallas as pl
from jax.experimental.pallas import tpu as pltpu

F32 = jnp.float32
BF16 = jnp.bfloat16

D_MODEL = 1024
DEPTH = 4
N_EVEN = 2
N_ODD = 2
NORM_EPS = 1e-6
PAGE_SIZE = 128

SSD_INNER = 2048
SSD_HEAD_DIM = 64
SSD_HEADS = 32
SSD_GROUPS = 4
SSD_STATE = 128
SSD_CONV = 4
XBC_DIM = SSD_INNER + 2 * SSD_GROUPS * SSD_STATE
SSD_Q = 128

MOBA_HEADS = 8
MOBA_HEAD_DIM = 128
MOBA_BLOCK = 256
MOBA_TOPK = 3
ROT_DIM = 32
ROPE_THETA = 500000.0

RWKV_HEAD_DIM = 64
RWKV_HEADS = 16
RWKV_GN_EPS = 64e-5
RWKV_SHIFT_DIM = 3 * D_MODEL + 64 + 64 + 128
RWKV_Q = 64

GLA_HEADS = 4
GLA_DK = 128
GLA_DV = 256
GLA_KEY_DIM = 512
GLA_GATE_RANK = 16
GLA_GATE_NORM = 16.0
GLA_Q = 64

N_MEM = 256
MEM_HEADS = 4
MEM_HEAD_DIM = 256
D_FF = 2816

LANES = 128
SUBLANES = 8
VMEM_LIMIT = 56 * 1024 * 1024


def _cp(sem, vmem=VMEM_LIMIT):
    return pltpu.CompilerParams(dimension_semantics=sem, vmem_limit_bytes=vmem)


def _bdot(a, b):
    return jnp.dot(a.astype(BF16), b.astype(BF16), preferred_element_type=F32)


def _bdot_nt(a, b):
    return lax.dot_general(a.astype(BF16), b.astype(BF16), (((1,), (1,)), ((), ())),
                           preferred_element_type=F32)


def _bdot_tn(a, b):
    k = a.shape[0]
    kp = -(-k // LANES) * LANES
    if kp != k:
        a = jnp.concatenate([a, jnp.zeros((kp - k, a.shape[1]), a.dtype)], axis=0)
        b = jnp.concatenate([b, jnp.zeros((kp - k, b.shape[1]), b.dtype)], axis=0)
    return jnp.dot(a.T.astype(BF16), b.astype(BF16), preferred_element_type=F32)


def _split3(x):
    hi = x.astype(BF16)
    r1 = x - hi.astype(F32)
    mid = r1.astype(BF16)
    lo = (r1 - mid.astype(F32)).astype(BF16)
    return hi, mid, lo


def _exact_left(m01, x):
    hi, mid, lo = _split3(x)
    m = m01.astype(BF16)
    return (jnp.dot(m, hi, preferred_element_type=F32) + jnp.dot(m, mid, preferred_element_type=F32)
            + jnp.dot(m, lo, preferred_element_type=F32))


def _exact_right(x, m01):
    hi, mid, lo = _split3(x)
    m = m01.astype(BF16)
    return (jnp.dot(hi, m, preferred_element_type=F32) + jnp.dot(mid, m, preferred_element_type=F32)
            + jnp.dot(lo, m, preferred_element_type=F32))


def _f32_nt(a, b):
    ah, am, al = _split3(a)
    bh, bm, bl = _split3(b)
    dn = (((1,), (1,)), ((), ()))
    d = lambda x, y: lax.dot_general(x, y, dn, preferred_element_type=F32)
    return d(ah, bh) + (d(ah, bm) + d(am, bh)) + (d(am, bm) + d(ah, bl) + d(al, bh))


def _rms(x, g, eps=NORM_EPS):
    return x * lax.rsqrt(jnp.mean(x * x, axis=-1, keepdims=True) + eps) * g


def _sigmoid(x):
    return 1.0 / (1.0 + jnp.exp(-x))


def _silu(x):
    return x * _sigmoid(x)


def _softplus(x):
    return jnp.maximum(x, 0.0) + jnp.log1p(jnp.exp(-jnp.abs(x)))


def _iota(shape, dim):
    return lax.broadcasted_iota(jnp.int32, shape, dim)


def _div(x, d):
    return lax.shift_right_logical(x, jnp.int32(int(math.log2(d))))


def _mod(x, d):
    return x & jnp.int32(d - 1)


def _pad_rows(x, rows):
    if x.shape[0] == rows:
        return x
    return jnp.concatenate([x, jnp.zeros((rows - x.shape[0], x.shape[1]), x.dtype)], axis=0)


def _seg_ones(seg):
    r = _div(_iota((LANES, LANES), 0), seg)
    c = _div(_iota((LANES, LANES), 1), seg)
    return (r == c).astype(BF16)


def _seg_sum(x, ones_bd):
    parts = []
    for s in range(x.shape[1] // LANES):
        parts.append(_exact_right(x[:, s * LANES:(s + 1) * LANES], ones_bd))
    return parts[0] if len(parts) == 1 else jnp.concatenate(parts, axis=1)


def _mm_kernel(*refs, n_a, has_gain, hn_dim, has_res):
    pos = 0
    a_refs = refs[pos:pos + n_a]
    pos += n_a
    g_ref = None
    if has_gain:
        g_ref = refs[pos]
        pos += 1
    w_refs = refs[pos:pos + n_a]
    pos += n_a
    hn_ref = None
    if hn_dim:
        hn_ref = refs[pos]
        pos += 1
    res_ref = None
    if has_res:
        res_ref = refs[pos]
        pos += 1
    o_ref = refs[pos]
    pos += 1
    if has_gain:
        an_ref = refs[pos]

        @pl.when(pl.program_id(1) == 0)
        def _():
            an_ref[...] = _rms(a_refs[0][...].astype(F32), g_ref[...]).astype(BF16)

        acc = jnp.dot(an_ref[...], w_refs[0][...], preferred_element_type=F32)
    else:
        acc = None
        for a_ref, w_ref in zip(a_refs, w_refs):
            d = jnp.dot(a_ref[...].astype(BF16), w_ref[...], preferred_element_type=F32)
            acc = d if acc is None else acc + d
    if hn_dim:
        parts = [_rms(acc[:, s * hn_dim:(s + 1) * hn_dim], hn_ref[...]) for s in range(acc.shape[1] // hn_dim)]
        acc = parts[0] if len(parts) == 1 else jnp.concatenate(parts, axis=1)
    if has_res:
        acc = acc + res_ref[...]
    o_ref[...] = acc.astype(o_ref.dtype)


def _mm(a_list, w_list, *, gain=None, head_gain=None, res=None, tm=1024, tn=1024, out_dtype=F32, name="mm"):
    m = a_list[0].shape[0]
    n = w_list[0].shape[1]
    tm = min(tm, m)
    tn = min(tn, n)
    assert m % tm == 0 and n % tn == 0, (m, tm, n, tn)
    has_gain = gain is not None
    assert not has_gain or len(a_list) == 1
    in_specs = [pl.BlockSpec((tm, a.shape[1]), lambda i, j: (i, 0)) for a in a_list]
    args = list(a_list)
    if has_gain:
        in_specs.append(pl.BlockSpec((1, a_list[0].shape[1]), lambda i, j: (0, 0)))
        args.append(gain.reshape(1, -1))
    for w in w_list:
        in_specs.append(pl.BlockSpec((w.shape[0], tn), lambda i, j: (0, j)))
        args.append(w)
    hn_dim = 0
    if head_gain is not None:
        hn_dim = head_gain.shape[-1]
        in_specs.append(pl.BlockSpec((1, hn_dim), lambda i, j: (0, 0)))
        args.append(head_gain.reshape(1, -1))
    if res is not None:
        in_specs.append(pl.BlockSpec((tm, tn), lambda i, j: (i, j)))
        args.append(res)
    scratch = [pltpu.VMEM((tm, a_list[0].shape[1]), BF16)] if has_gain else []
    return pl.pallas_call(
        functools.partial(_mm_kernel, n_a=len(a_list), has_gain=has_gain, hn_dim=hn_dim, has_res=res is not None),
        grid=(m // tm, n // tn),
        in_specs=in_specs,
        out_specs=pl.BlockSpec((tm, tn), lambda i, j: (i, j)),
        out_shape=jax.ShapeDtypeStruct((m, n), out_dtype),
        scratch_shapes=scratch,
        compiler_params=_cp(("parallel", "arbitrary")),
        name=name,
    )(*args)


def _ffn_kernel(x_ref, g_ref, wg_ref, wu_ref, wd_ref, o_ref, an_ref, acc_ref):
    f = pl.program_id(1)

    @pl.when(f == 0)
    def _():
        an_ref[...] = _rms(x_ref[...], g_ref[...]).astype(BF16)
        acc_ref[...] = jnp.zeros_like(acc_ref)

    a = an_ref[...]
    g = jnp.dot(a, wg_ref[...], preferred_element_type=F32)
    u = jnp.dot(a, wu_ref[...], preferred_element_type=F32)
    act = (_silu(g) * u).astype(BF16)
    acc_ref[...] += jnp.dot(act, wd_ref[...], preferred_element_type=F32)

    @pl.when(f == pl.num_programs(1) - 1)
    def _():
        o_ref[...] = x_ref[...] + acc_ref[...]


def _ffn(x, gain, wg, wu, wd, *, tm=1024, tf=256):
    m = x.shape[0]
    tm = min(tm, m)
    nf = D_FF // tf
    return pl.pallas_call(
        _ffn_kernel,
        grid=(m // tm, nf),
        in_specs=[pl.BlockSpec((tm, D_MODEL), lambda i, f: (i, 0)),
                  pl.BlockSpec((1, D_MODEL), lambda i, f: (0, 0)),
                  pl.BlockSpec((D_MODEL, tf), lambda i, f: (0, f)),
                  pl.BlockSpec((D_MODEL, tf), lambda i, f: (0, f)),
                  pl.BlockSpec((tf, D_MODEL), lambda i, f: (f, 0))],
        out_specs=pl.BlockSpec((tm, D_MODEL), lambda i, f: (i, 0)),
        out_shape=jax.ShapeDtypeStruct((m, D_MODEL), F32),
        scratch_shapes=[pltpu.VMEM((tm, D_MODEL), BF16), pltpu.VMEM((tm, D_MODEL), F32)],
        compiler_params=_cp(("parallel", "arbitrary")),
        name="ffn",
    )(x, gain.reshape(1, -1), wg, wu, wd)


def _qkprep_kernel(q_ref, k_ref, cos_ref, sin_ref, gq_ref, gk_ref, qo_ref, ko_ref, *maybe_km, with_kmean):
    cos = cos_ref[...]
    sin = sin_ref[...]
    lane = _iota(cos.shape, 1)
    half = ROT_DIM // 2

    def prep(x_ref, g_ref, o_ref):
        for h in range(MOBA_HEADS):
            sl = slice(h * MOBA_HEAD_DIM, (h + 1) * MOBA_HEAD_DIM)
            y = _rms(x_ref[:, sl], g_ref[...])
            sw = jnp.where(lane < half, pltpu.roll(y, MOBA_HEAD_DIM - half, 1), pltpu.roll(y, half, 1))
            o_ref[:, sl] = y * cos + sw * sin

    prep(q_ref, gq_ref, qo_ref)
    prep(k_ref, gk_ref, ko_ref)
    if with_kmean:
        maybe_km[0][0] = jnp.mean(ko_ref[...], axis=0, keepdims=True)


def _rope_tables(pos):
    half = ROT_DIM // 2
    inv_freq = ROPE_THETA ** (-jnp.arange(half, dtype=F32) / half)
    ang = pos.astype(F32)[:, None] * inv_freq[None, :]
    cos, sin = jnp.cos(ang), jnp.sin(ang)
    ones = jnp.ones((pos.shape[0], MOBA_HEAD_DIM - ROT_DIM), F32)
    cos_t = jnp.concatenate([cos, cos, ones], axis=1)
    sin_t = jnp.concatenate([-sin, sin, 0.0 * ones], axis=1)
    return cos_t, sin_t


def _qk_prep(proj, q_col, k_col, cos_t, sin_t, gq, gk, *, tq, with_kmean):
    t = proj.shape[0]
    n_tab = cos_t.shape[0] // tq
    out_shape = [jax.ShapeDtypeStruct((t, D_MODEL), F32), jax.ShapeDtypeStruct((t, D_MODEL), F32)]
    out_specs = [pl.BlockSpec((tq, D_MODEL), lambda i: (i, 0)), pl.BlockSpec((tq, D_MODEL), lambda i: (i, 0))]
    if with_kmean:
        out_shape.append(jax.ShapeDtypeStruct((t // tq, 1, D_MODEL), F32))
        out_specs.append(pl.BlockSpec((1, 1, D_MODEL), lambda i: (i, 0, 0)))
    return pl.pallas_call(
        functools.partial(_qkprep_kernel, with_kmean=with_kmean),
        grid=(t // tq,),
        in_specs=[pl.BlockSpec((tq, D_MODEL), lambda i: (i, q_col)),
                  pl.BlockSpec((tq, D_MODEL), lambda i: (i, k_col)),
                  pl.BlockSpec((tq, MOBA_HEAD_DIM), lambda i: (i % n_tab, 0)),
                  pl.BlockSpec((tq, MOBA_HEAD_DIM), lambda i: (i % n_tab, 0)),
                  pl.BlockSpec((1, MOBA_HEAD_DIM), lambda i: (0, 0)),
                  pl.BlockSpec((1, MOBA_HEAD_DIM), lambda i: (0, 0))],
        out_specs=out_specs,
        out_shape=out_shape,
        compiler_params=_cp(("parallel",)),
        name="qk_prep",
    )(proj, proj, cos_t, sin_t, gq.reshape(1, -1), gk.reshape(1, -1))


def _moba_prompt_kernel(q_ref, k_ref, v_ref, km_ref, o_ref, *, nb, n_sel):
    qi = pl.program_id(2)
    blk = MOBA_BLOCK
    scale = MOBA_HEAD_DIM ** -0.5
    q = q_ref[...]
    qb = (q * scale).astype(BF16)
    gt = _f32_nt(_pad_rows(km_ref[0], LANES), q)[0:SUBLANES]
    rown = _iota((SUBLANES, blk), 0)
    past = rown < qi
    sel_t = jnp.zeros((SUBLANES, blk), F32)
    for n in range(nb):
        g_n = gt[n:n + 1, :]
        beats = past & ((gt > g_n) | ((gt == g_n) & (rown < n)))
        sel_n = (jnp.sum(beats.astype(F32), axis=0, keepdims=True) < n_sel).astype(F32)
        sel_t = jnp.where(rown == n, sel_n, sel_t)
    sel_m = _pad_rows(sel_t, LANES).T
    ci = _iota((blk, blk), 1)
    rw = _iota((blk, LANES), 0)
    cn = _iota((blk, LANES), 1)
    thr = jnp.where(cn == qi, rw, jnp.where((sel_m > 0.5) & (cn < qi), blk - 1, -1))

    def attend(nblocks):
        s = _bdot_nt(qb, k_ref[0:nblocks * blk, :])
        parts = []
        for n in range(nblocks):
            parts.append(jnp.where(ci <= thr[:, n:n + 1], s[:, n * blk:(n + 1) * blk], -jnp.inf))
        s = parts[0] if nblocks == 1 else jnp.concatenate(parts, axis=1)
        m = jnp.max(s, axis=-1, keepdims=True)
        p = jnp.exp(s - m)
        l = jnp.sum(p, axis=-1, keepdims=True)
        o_ref[...] = _bdot(p, v_ref[0:nblocks * blk, :]) / l

    stride = max(nb // 4, 1)
    sizes = list(range(stride, nb, stride)) + [nb]
    lo = 0
    for size in sizes:
        pl.when((qi >= lo) & (qi < size))(functools.partial(attend, size))
        lo = size


def _moba_prompt(q, k, v, kmean, *, b, l):
    nb = l // MOBA_BLOCK
    assert 1 <= nb <= SUBLANES
    n_sel = min(MOBA_TOPK, (l - 1) // MOBA_BLOCK)
    hd = MOBA_HEAD_DIM
    return pl.pallas_call(
        functools.partial(_moba_prompt_kernel, nb=nb, n_sel=n_sel),
        grid=(b, MOBA_HEADS, nb),
        in_specs=[pl.BlockSpec((MOBA_BLOCK, hd), lambda bi, h, qi: (bi * nb + qi, h)),
                  pl.BlockSpec((l, hd), lambda bi, h, qi: (bi, h)),
                  pl.BlockSpec((l, hd), lambda bi, h, qi: (bi, h)),
                  pl.BlockSpec((1, nb, hd), lambda bi, h, qi: (bi, 0, h))],
        out_specs=pl.BlockSpec((MOBA_BLOCK, hd), lambda bi, h, qi: (bi * nb + qi, h)),
        out_shape=jax.ShapeDtypeStruct((b * l, D_MODEL), F32),
        compiler_params=_cp(("parallel", "parallel", "arbitrary")),
        name="moba_prompt",
    )(q, k, v, kmean)


def _moba_sample_kernel(pt_ref, q_ref, kn_ref, vn_ref, *rest, n_pages, pps, l_real, n_sel):
    del pt_ref
    kc_refs, vc_refs = rest[0:pps], rest[pps:2 * pps]
    o_ref, m_s, l_s, acc_s, ks_s = rest[2 * pps:]
    step = pl.program_id(1)
    nh = MOBA_HEADS
    rows = l_real * nh
    scale = MOBA_HEAD_DIM ** -0.5
    pages_per_block = MOBA_BLOCK // PAGE_SIZE
    q = q_ref[0]
    width = PAGE_SIZE * nh
    same_head = _mod(_iota((rows, width), 0), nh) == _mod(_iota((rows, width), 1), nh)
    kps = [r[0] for r in kc_refs]
    ss = [jnp.where(same_head, _bdot_nt(q, kp) * scale, -jnp.inf) for kp in kps]
    ms = [jnp.max(s, axis=-1, keepdims=True) for s in ss]
    prs = [jnp.exp(s - m) for s, m in zip(ss, ms)]
    accs = [_bdot(pr, r[0]) for pr, r in zip(prs, vc_refs)]
    ksums = [jnp.sum(kp.reshape(PAGE_SIZE, nh, MOBA_HEAD_DIM), axis=0) for kp in kps]
    for i in range(pps):
        pg = step * pps + i
        m_s[pg] = ms[i]
        l_s[pg] = jnp.sum(prs[i], axis=-1, keepdims=True)
        acc_s[pg] = accs[i]
    for j in range(pps // pages_per_block):
        tot = ksums[j * pages_per_block]
        for i in range(1, pages_per_block):
            tot = tot + ksums[j * pages_per_block + i]
        ks_s[step * (pps // pages_per_block) + j] = tot

    @pl.when(step == n_pages // pps - 1)
    def _():
        n_blocks = n_pages // pages_per_block
        gates = []
        for n in range(n_blocks):
            kmean = ks_s[n] * (1.0 / MOBA_BLOCK)
            gates.append(jnp.sum(q * jnp.concatenate([kmean] * l_real, axis=0), axis=-1, keepdims=True))
        sels = []
        for n in range(n_blocks):
            rank = jnp.zeros((rows, 1), F32)
            for j in range(n_blocks):
                if j != n:
                    beats = (gates[j] > gates[n]) | ((gates[j] == gates[n]) & (j < n))
                    rank = rank + beats.astype(F32)
            sels.append(rank < n_sel)
        kn = _pad_rows(kn_ref[0], LANES)
        vn = _pad_rows(vn_ref[0], LANES)
        ro = _iota((rows, LANES), 0)
        co = _iota((rows, LANES), 1)
        vis = (_mod(ro, nh) == _mod(co, nh)) & (_div(co, nh) <= _div(ro, nh))
        s_own = jnp.where(vis, _bdot_nt(q, kn) * scale, -jnp.inf)
        big = jnp.max(s_own, axis=-1, keepdims=True)
        for pg in range(n_pages):
            big = jnp.maximum(big, jnp.where(sels[pg // pages_per_block], m_s[pg], -jnp.inf))
        p_own = jnp.exp(s_own - big)
        den = jnp.sum(p_own, axis=-1, keepdims=True)
        num = _bdot(p_own, vn)
        for pg in range(n_pages):
            w = jnp.where(sels[pg // pages_per_block], jnp.exp(m_s[pg] - big), 0.0)
            den = den + w * l_s[pg]
            num = num + w * acc_s[pg]
        o_ref[0] = num / den


def _moba_sample(page_table, q, k_new, v_new, cache_k, cache_v, *, layer_off):
    b, n_pages = page_table.shape
    rows = q.shape[1]
    l_real = rows // MOBA_HEADS
    q0 = n_pages * PAGE_SIZE
    assert q0 % MOBA_BLOCK == 0 and rows <= LANES
    n_sel = min(MOBA_TOPK, (q0 + l_real - 1) // MOBA_BLOCK)
    hd = MOBA_HEAD_DIM
    pps = 8
    assert n_pages % pps == 0 and pps % (MOBA_BLOCK // PAGE_SIZE) == 0
    tok_spec = pl.BlockSpec((1, rows, hd), lambda bi, p, pt: (bi, 0, 0))

    def page_spec(i):
        return pl.BlockSpec((1, PAGE_SIZE * MOBA_HEADS, hd),
                            lambda bi, p, pt: (layer_off + pt[bi, p * pps + i], 0, 0))

    grid_spec = pltpu.PrefetchScalarGridSpec(
        num_scalar_prefetch=1,
        grid=(b, n_pages // pps),
        in_specs=([tok_spec, tok_spec, tok_spec] + [page_spec(i) for i in range(pps)]
                  + [page_spec(i) for i in range(pps)]),
        out_specs=tok_spec,
        scratch_shapes=[pltpu.VMEM((n_pages, rows, 1), F32),
                        pltpu.VMEM((n_pages, rows, 1), F32),
                        pltpu.VMEM((n_pages, rows, hd), F32),
                        pltpu.VMEM((n_pages * PAGE_SIZE // MOBA_BLOCK, MOBA_HEADS, hd), F32)],
    )
    return pl.pallas_call(
        functools.partial(_moba_sample_kernel, n_pages=n_pages, pps=pps, l_real=l_real, n_sel=n_sel),
        grid_spec=grid_spec,
        out_shape=jax.ShapeDtypeStruct((b, rows, hd), F32),
        compiler_params=_cp(("parallel", "arbitrary")),
        name="moba_sample",
    )(page_table, q, k_new, v_new, *([cache_k] * pps), *([cache_v] * pps))


def _ssd_kernel(*refs, lb, has_state):
    if has_state:
        (z_ref, x_ref, b_ref, c_ref, dt_ref, dtt_ref, arow_ref, acol_ref, cwx_ref, cwb_ref, cwc_ref,
         cbx_ref, cbb_ref, cbc_ref, dl_ref, gs_ref, px_ref, pb_ref, pc_ref, s0_ref,
         y_ref, so_ref, s_s, tx_s, tb_s, tc_s) = refs
    else:
        (z_ref, x_ref, b_ref, c_ref, dt_ref, dtt_ref, arow_ref, acol_ref, cwx_ref, cwb_ref, cwc_ref,
         cbx_ref, cbb_ref, cbc_ref, dl_ref, gs_ref,
         y_ref, so_ref, s_s, tx_s, tb_s, tc_s) = refs
    c = pl.program_id(1)
    q = SSD_Q
    kc = SSD_CONV

    @pl.when(c == 0)
    def _():
        if has_state:
            s_s[...] = s0_ref[0]
            for t_s, p_ref in ((tx_s, px_ref), (tb_s, pb_ref), (tc_s, pc_ref)):
                t_s[...] = jnp.zeros_like(t_s)
                t_s[8 - (kc - 1):8, :] = p_ref[0]
        else:
            s_s[...] = jnp.zeros_like(s_s)
            tx_s[...] = jnp.zeros_like(tx_s)
            tb_s[...] = jnp.zeros_like(tb_s)
            tc_s[...] = jnp.zeros_like(tc_s)

    def conv(x, tail_s, w_ref, bias_ref):
        prev8 = tail_s[...]
        row8 = _iota((8, x.shape[1]), 0)
        out = x * w_ref[kc - 1:kc, :]
        for j in range(1, kc):
            xs = pltpu.roll(x, j, 0)
            head = jnp.where(row8 < j, pltpu.roll(prev8, j, 0), xs[0:8])
            sh = head if lb == 8 else jnp.concatenate([head, xs[8:]], axis=0)
            out = out + sh * w_ref[kc - 1 - j:kc - j, :]
        tail_s[...] = x[lb - 8:lb]
        return _silu(out + bias_ref[...])

    xc = _pad_rows(conv(x_ref[0], tx_s, cwx_ref, cbx_ref), q)
    bc_all = _pad_rows(conv(b_ref[0], tb_s, cwb_ref, cbb_ref), q)
    cc_all = _pad_rows(conv(c_ref[0], tc_s, cwc_ref, cbc_ref), q)
    ri = _iota((q, q), 0)
    ci = _iota((q, q), 1)
    causal = ri >= ci
    upper = ri <= ci
    lane = _iota((q, LANES), 1)
    low = lane < SSD_HEAD_DIM
    row128 = _iota((2 * SSD_HEAD_DIM, 1), 0)
    grps = range(SSD_GROUPS)
    prs = [(g, pp) for g in grps for pp in range(4)]
    dts = [dt_ref[0, g] for g in grps]
    dtts = [dtt_ref[0, g] for g in grps]
    css = [_exact_left(causal, dts[g] * -jnp.exp(arow_ref[g])) for g in grps]
    csts = [_exact_right(dtts[g] * -jnp.exp(acol_ref[g]), upper) for g in grps]
    bcs = [bc_all[:, g * LANES:(g + 1) * LANES] for g in grps]
    ccs = [cc_all[0:lb, g * LANES:(g + 1) * LANES] for g in grps]
    cbs = [_bdot_nt(ccs[g], bcs[g]) for g in grps]
    xps = [xc[:, (4 * g + pp) * LANES:(4 * g + pp + 1) * LANES] for g, pp in prs]
    sps = [s_s[4 * g + pp] for g, pp in prs]
    inter = [_bdot_nt(ccs[g], sps[i]) for i, (g, pp) in enumerate(prs)]

    def w_head(g, h):
        dec = jnp.where(causal[0:lb], jnp.exp(css[g][0:lb, h:h + 1] - csts[g][h:h + 1, :]), 0.0)
        return cbs[g] * dec * dtts[g][h:h + 1, :]

    intra = [_bdot(w_head(g, 2 * pp), jnp.where(low, xps[i], 0.0))
             + _bdot(w_head(g, 2 * pp + 1), jnp.where(low, 0.0, xps[i])) for i, (g, pp) in enumerate(prs)]
    csa = [css[g][:, 2 * pp:2 * pp + 1] for g, pp in prs]
    csb = [css[g][:, 2 * pp + 1:2 * pp + 2] for g, pp in prs]
    tails = [jnp.where(low, jnp.exp(csa[i][q - 1:q, :] - csa[i]) * dts[g][:, 2 * pp:2 * pp + 1],
                       jnp.exp(csb[i][q - 1:q, :] - csb[i]) * dts[g][:, 2 * pp + 1:2 * pp + 2])
             for i, (g, pp) in enumerate(prs)]
    upds = [_bdot_tn(xps[i] * tails[i], bcs[g]) for i, (g, pp) in enumerate(prs)]
    ys = []
    for i, (g, pp) in enumerate(prs):
        s_s[i] = sps[i] * jnp.where(row128 < SSD_HEAD_DIM, jnp.exp(csa[i][q - 1:q, :]),
                                    jnp.exp(csb[i][q - 1:q, :])) + upds[i]
        ys.append(intra[i] + inter[i] * jnp.where(low[0:lb], jnp.exp(csa[i][0:lb]), jnp.exp(csb[i][0:lb]))
                  + dl_ref[:, i * LANES:(i + 1) * LANES] * xps[i][0:lb])
    gw = SSD_INNER // SSD_GROUPS
    for g in grps:
        y = jnp.concatenate(ys[4 * g:4 * g + 4], axis=1) * _silu(z_ref[0, :, g * gw:(g + 1) * gw])
        y_ref[0, :, g * gw:(g + 1) * gw] = _rms(y, gs_ref[:, g * gw:(g + 1) * gw])

    @pl.when(c == pl.num_programs(1) - 1)
    def _():
        so_ref[0] = s_s[...]


def _ssd(pm, dt, dtt, conv_w, conv_b, a_log, d_skip, g_ssd, conv_prev, s0, *, lb, state_off=0):
    b, l, _ = pm.shape
    nc = l // lb
    hg = SSD_HEADS // SSD_GROUPS
    gw = SSD_INNER // SSD_GROUPS
    has_state = s0 is not None
    cwx, cwb, cwc = conv_w[:, :SSD_INNER], conv_w[:, SSD_INNER:SSD_INNER + 512], conv_w[:, SSD_INNER + 512:]
    cbx, cbb, cbc = (conv_b[None, :SSD_INNER], conv_b[None, SSD_INNER:SSD_INNER + 512],
                     conv_b[None, SSD_INNER + 512:])
    a_pad = jnp.pad(a_log.reshape(SSD_GROUPS, 1, hg), ((0, 0), (0, 0), (0, LANES - hg)))
    a_col = a_log.reshape(SSD_GROUPS, hg, 1)
    d_lane = jnp.repeat(d_skip, SSD_HEAD_DIM)[None, :]
    gn = SSD_GROUPS * SSD_STATE
    npair = SSD_HEADS // 2
    full = lambda shape: pl.BlockSpec(shape, lambda bi, c: (0,) * len(shape))
    in_specs = [
        pl.BlockSpec((1, lb, SSD_INNER), lambda bi, c: (bi, c, 0)),
        pl.BlockSpec((1, lb, SSD_INNER), lambda bi, c: (bi, c, 1)),
        pl.BlockSpec((1, lb, gn), lambda bi, c: (bi, c, 8)),
        pl.BlockSpec((1, lb, gn), lambda bi, c: (bi, c, 9)),
        pl.BlockSpec((1, SSD_GROUPS, SSD_Q, LANES), lambda bi, c: (bi, 0, c, 0)),
        pl.BlockSpec((1, SSD_GROUPS, hg, SSD_Q), lambda bi, c: (bi, 0, 0, c)),
        full((SSD_GROUPS, 1, LANES)), full((SSD_GROUPS, hg, 1)),
        full((SSD_CONV, SSD_INNER)), full((SSD_CONV, gn)), full((SSD_CONV, gn)),
        full((1, SSD_INNER)), full((1, gn)), full((1, gn)),
        full((1, SSD_INNER)), full((1, SSD_INNER)),
    ]
    args = [pm, pm, pm, pm, dt, dtt, a_pad, a_col, cwx, cwb, cwc, cbx, cbb, cbc, d_lane, g_ssd[None, :]]
    if has_state:
        in_specs += [
            pl.BlockSpec((1, SSD_CONV - 1, SSD_INNER), lambda bi, c: (state_off + bi, 0, 0)),
            pl.BlockSpec((1, SSD_CONV - 1, gn), lambda bi, c: (state_off + bi, 0, 4)),
            pl.BlockSpec((1, SSD_CONV - 1, gn), lambda bi, c: (state_off + bi, 0, 5)),
            pl.BlockSpec((1, npair, LANES, LANES), lambda bi, c: (state_off + bi, 0, 0, 0)),
        ]
        args += [conv_prev, conv_prev, conv_prev, s0]
    y, s_out = pl.pallas_call(
        functools.partial(_ssd_kernel, lb=lb, has_state=has_state),
        grid=(b, nc),
        in_specs=in_specs,
        out_specs=[pl.BlockSpec((1, lb, SSD_INNER), lambda bi, c: (bi, c, 0)),
                   pl.BlockSpec((1, npair, LANES, LANES), lambda bi, c: (bi, 0, 0, 0))],
        out_shape=[jax.ShapeDtypeStruct((b, l, SSD_INNER), F32),
                   jax.ShapeDtypeStruct((b, npair, LANES, LANES), F32)],
        scratch_shapes=[pltpu.VMEM((npair, LANES, LANES), F32), pltpu.VMEM((8, SSD_INNER), F32),
                        pltpu.VMEM((8, gn), F32), pltpu.VMEM((8, gn), F32)],
        compiler_params=_cp(("parallel", "arbitrary")),
        name="ssd",
    )(*args)
    return y, s_out


def _dt_kernel(x_ref, g_ref, w_ref, wt_ref, bias_ref, biast_ref, o_ref, ot_ref):
    a = _rms(x_ref[...], g_ref[...]).astype(BF16)
    o_ref[...] = _softplus(jnp.dot(a, w_ref[...], preferred_element_type=F32) + bias_ref[...])
    t = lax.dot_general(wt_ref[...], a, (((1,), (1,)), ((), ())), preferred_element_type=F32)
    ot_ref[...] = _softplus(t + biast_ref[...])


def _dt_proj(x, gain, w_dt, dt_bias, *, tm=1024):
    m = x.shape[0]
    tm = min(tm, m)
    w_pad = jnp.pad(w_dt, ((0, 0), (0, LANES - SSD_HEADS))).astype(BF16)
    wt = w_dt.T.astype(BF16)
    bias_pad = jnp.pad(dt_bias, (0, LANES - SSD_HEADS))[None, :]
    return pl.pallas_call(
        _dt_kernel,
        grid=(m // tm,),
        in_specs=[pl.BlockSpec((tm, D_MODEL), lambda i: (i, 0)),
                  pl.BlockSpec((1, D_MODEL), lambda i: (0, 0)),
                  pl.BlockSpec((D_MODEL, LANES), lambda i: (0, 0)),
                  pl.BlockSpec((SSD_HEADS, D_MODEL), lambda i: (0, 0)),
                  pl.BlockSpec((1, LANES), lambda i: (0, 0)),
                  pl.BlockSpec((SSD_HEADS, 1), lambda i: (0, 0))],
        out_specs=[pl.BlockSpec((tm, LANES), lambda i: (i, 0)),
                   pl.BlockSpec((SSD_HEADS, tm), lambda i: (0, i))],
        out_shape=[jax.ShapeDtypeStruct((m, LANES), F32), jax.ShapeDtypeStruct((SSD_HEADS, m), F32)],
        compiler_params=_cp(("parallel",)),
        name="dt_proj",
    )(x, gain.reshape(1, -1), w_pad, wt, bias_pad, dt_bias[:, None])


def _gla_levels(n_valid):
    s, levels = 1, []
    while s < n_valid:
        levels.append(s)
        s *= 2
    return levels[::-1]


def _gla_kernel(*refs, lb, n_valid, has_state):
    if has_state:
        (v_ref, gg_ref, q_ref, k_ref, gf_ref, wf_ref, bf_ref, gn_ref, s0_ref, o_ref, so_ref, st_s) = refs
    else:
        (v_ref, gg_ref, q_ref, k_ref, gf_ref, wf_ref, bf_ref, gn_ref, o_ref, so_ref, st_s) = refs
    c = pl.program_id(1)
    qn = GLA_Q
    heads = range(GLA_HEADS)
    dk, dv = GLA_DK, GLA_DV

    @pl.when(c == 0)
    def _():
        for h in heads:
            if has_state:
                st_s[h] = s0_ref[0, h].T
            else:
                st_s[h] = jnp.zeros((dv, dk), F32)

    logits = _bdot(_pad_rows(gf_ref[0], qn), wf_ref[...]) + bf_ref[...]
    logf = -_softplus(-logits) * (1.0 / GLA_GATE_NORM)
    logf = jnp.where(_iota(logf.shape, 0) < n_valid, logf, 0.0)

    ri = _iota((qn, qn), 0)
    ci = _iota((qn, qn), 1)
    levels = _gla_levels(min(n_valid, qn))
    mats = [ri >= ci]
    masks = []
    for s in levels:
        same = _div(ri, s) == _div(ci, s)
        mats.append(same & (ri >= ci))
        mats.append(same)
        masks.append((_div(ri, 2 * s) == _div(ci, 2 * s)) & (_mod(_div(ri, s), 2) == 1)
                     & (_mod(_div(ci, s), 2) == 0))
    stack = jnp.concatenate([mm.astype(BF16) for mm in mats], axis=0)
    cums = _exact_left(stack, logf)

    qs = [_pad_rows(q_ref[0, :, h * dk:(h + 1) * dk], qn) * (dk ** -0.5) for h in heads]
    ks = [_pad_rows(k_ref[0, :, h * dk:(h + 1) * dk], qn) for h in heads]
    vs = [_pad_rows(v_ref[0, :, h * dv:(h + 1) * dv], qn) for h in heads]
    cbs = [cums[0:qn, h * dk:(h + 1) * dk] for h in heads]
    sts = [st_s[h] for h in heads]
    inter = [_bdot_nt(qs[h] * jnp.exp(cbs[h]), sts[h]) for h in heads]
    atts = [jnp.where(ri == ci, jnp.sum(qs[h] * ks[h], axis=-1, keepdims=True), 0.0) for h in heads]
    for idx in range(len(levels)):
        prods = []
        for h in heads:
            lq = cums[(1 + 2 * idx) * qn:(2 + 2 * idx) * qn, h * dk:(h + 1) * dk]
            bs = cums[(2 + 2 * idx) * qn:(3 + 2 * idx) * qn, h * dk:(h + 1) * dk]
            prods.append(_bdot_nt(qs[h] * jnp.exp(lq), ks[h] * jnp.exp(bs - lq)))
        atts = [atts[h] + jnp.where(masks[idx], prods[h], 0.0) for h in heads]
    os_ = [_bdot(atts[h], vs[h]) + inter[h] for h in heads]
    cbls = [cbs[h][qn - 1:qn, :] for h in heads]
    upds = [_bdot_tn(vs[h], ks[h] * jnp.exp(cbls[h] - cbs[h])) for h in heads]
    for h in heads:
        st_s[h] = sts[h] * jnp.exp(cbls[h]) + upds[h]
        o_ref[0, :, h * dv:(h + 1) * dv] = (_rms(os_[h][0:lb], gn_ref[...])
                                            * _silu(gg_ref[0, :, h * dv:(h + 1) * dv]))

    @pl.when(c == pl.num_programs(1) - 1)
    def _():
        for h in heads:
            so_ref[0, h] = st_s[h].T


def _gla(pg, gf, w_f2, b_f, g_gla, s0, *, lb, n_valid, state_off=0):
    b, l, _ = pg.shape
    nc = l // lb
    has_state = s0 is not None
    wf_pad = jnp.pad(w_f2, ((0, LANES - GLA_GATE_RANK), (0, 0))).astype(BF16)
    hv = GLA_HEADS * GLA_DV
    hk = GLA_HEADS * GLA_DK
    st_spec = pl.BlockSpec((1, GLA_HEADS, GLA_DK, GLA_DV), lambda bi, c: (bi, 0, 0, 0))
    in_specs = [
        pl.BlockSpec((1, lb, hv), lambda bi, c: (bi, c, 0)),
        pl.BlockSpec((1, lb, hv), lambda bi, c: (bi, c, 1)),
        pl.BlockSpec((1, lb, hk), lambda bi, c: (bi, c, 4)),
        pl.BlockSpec((1, lb, hk), lambda bi, c: (bi, c, 5)),
        pl.BlockSpec((1, lb, LANES), lambda bi, c: (bi, c, 0)),
        pl.BlockSpec((LANES, hk), lambda bi, c: (0, 0)),
        pl.BlockSpec((1, hk), lambda bi, c: (0, 0)),
        pl.BlockSpec((1, GLA_DV), lambda bi, c: (0, 0)),
    ]
    args = [pg, pg, pg, pg, gf, wf_pad, b_f[None, :], g_gla[None, :]]
    if has_state:
        in_specs.append(pl.BlockSpec((1, GLA_HEADS, GLA_DK, GLA_DV), lambda bi, c: (state_off + bi, 0, 0, 0)))
        args.append(s0)
    return pl.pallas_call(
        functools.partial(_gla_kernel, lb=lb, n_valid=n_valid, has_state=has_state),
        grid=(b, nc),
        in_specs=in_specs,
        out_specs=[pl.BlockSpec((1, lb, hv), lambda bi, c: (bi, c, 0)), st_spec],
        out_shape=[jax.ShapeDtypeStruct((b, l, hv), F32),
                   jax.ShapeDtypeStruct((b, GLA_HEADS, GLA_DK, GLA_DV), F32)],
        scratch_shapes=[pltpu.VMEM((GLA_HEADS, GLA_DV, GLA_DK), F32)],
        compiler_params=_cp(("parallel", "arbitrary")),
        name="gla",
    )(*args)


def _rwkv_prep_kernel(*refs, tiles_per_seq, prev_given):
    if prev_given:
        (cur_ref, prev_ref, mu_ref, w12_ref, wg2_ref, w0_ref, a0_ref, kk_ref, ka_ref,
         r_o, lw_o, k_o, v_o, kk_o, a_o, g_o) = refs
    else:
        (cur_ref, mu_ref, w12_ref, wg2_ref, w0_ref, a0_ref, kk_ref, ka_ref,
         r_o, lw_o, k_o, v_o, kk_o, a_o, g_o, last_s) = refs
    d = D_MODEL
    cur = cur_ref[...]
    if prev_given:
        prev = prev_ref[...]
    else:
        i = pl.program_id(0)

        @pl.when(i % tiles_per_seq == 0)
        def _():
            last_s[...] = jnp.zeros_like(last_s)

        row = _iota(cur.shape, 0)
        first = jnp.broadcast_to(last_s[7:8, :], cur.shape)
        prev = jnp.where(row == 0, first, pltpu.roll(cur, 1, 0))
        last_s[...] = cur[cur.shape[0] - 8:]
    mixed = cur + (prev - cur) * mu_ref[...]
    r = mixed[:, 0:d]
    kc = mixed[:, d:2 * d]
    vc = mixed[:, 2 * d:3 * d]
    t1 = mixed[:, 3 * d:3 * d + LANES]
    lane = _iota(t1.shape, 1)
    t1 = jnp.where(lane < 64, jnp.tanh(t1), t1)
    lora = _bdot(t1, w12_ref[...])
    logw = -_softplus(-(w0_ref[...] + lora[:, 0:d])) - 0.5
    a = _sigmoid(a0_ref[...] + lora[:, d:2 * d])
    g = _bdot(_sigmoid(mixed[:, 3 * d + LANES:3 * d + 2 * LANES]), wg2_ref[...])
    kk = kc * kk_ref[...]
    ss = _seg_sum(kk * kk, _seg_ones(RWKV_HEAD_DIM))
    kk = kk * lax.rsqrt(jnp.maximum(ss, 1e-24))
    r_o[...] = r
    lw_o[...] = -jnp.exp(logw)
    k_o[...] = kc * (1.0 + (a - 1.0) * ka_ref[...])
    v_o[...] = vc
    kk_o[...] = kk
    a_o[...] = a
    g_o[...] = g


def _rwkv_prep(cur, prev, mu, w12, wg2, w0, a0, k_k, k_a, *, tm, seq_len):
    t = cur.shape[0]
    tm = min(tm, t)
    prev_given = prev is not None
    row_spec = pl.BlockSpec((tm, RWKV_SHIFT_DIM), lambda i: (i, 0))
    vec = lambda n: pl.BlockSpec((1, n), lambda i: (0, 0))
    in_specs = [row_spec] + ([row_spec] if prev_given else []) + [
        vec(RWKV_SHIFT_DIM),
        pl.BlockSpec((LANES, 2 * D_MODEL), lambda i: (0, 0)),
        pl.BlockSpec((LANES, D_MODEL), lambda i: (0, 0)),
        vec(D_MODEL), vec(D_MODEL), vec(D_MODEL), vec(D_MODEL)]
    args = [cur] + ([prev] if prev_given else []) + [mu[None, :], w12, wg2, w0[None, :], a0[None, :],
                                                    k_k[None, :], k_a[None, :]]
    out_spec = pl.BlockSpec((tm, D_MODEL), lambda i: (i, 0))
    return pl.pallas_call(
        functools.partial(_rwkv_prep_kernel, tiles_per_seq=max(seq_len // tm, 1), prev_given=prev_given),
        grid=(t // tm,),
        in_specs=in_specs,
        out_specs=[out_spec] * 7,
        out_shape=[jax.ShapeDtypeStruct((t, D_MODEL), F32)] * 7,
        scratch_shapes=[] if prev_given else [pltpu.VMEM((8, RWKV_SHIFT_DIM), F32)],
        compiler_params=_cp(("arbitrary",)),
        name="rwkv_prep",
    )(*args)


def _rwkv_kernel(*refs, lb, has_state):
    if has_state:
        (r_ref, lw_ref, k_ref, v_ref, kk_ref, a_ref, g_ref, rk_ref, gw_ref, gb_ref, s0_ref,
         y_ref, so_ref, s_s) = refs
    else:
        (r_ref, lw_ref, k_ref, v_ref, kk_ref, a_ref, g_ref, rk_ref, gw_ref, gb_ref,
         y_ref, so_ref, s_s) = refs
    c = pl.program_id(1)
    qn = RWKV_Q
    hd = RWKV_HEAD_DIM

    @pl.when(c == 0)
    def _():
        if has_state:
            s_s[...] = s0_ref[0]
        else:
            s_s[...] = jnp.zeros_like(s_s)

    ri = _iota((qn, qn), 0)
    ci = _iota((qn, qn), 1)
    lw_all = _pad_rows(lw_ref[0], qn)
    cw_all = _exact_left(ri >= ci, lw_all)
    lane = _iota((qn, LANES), 1)
    low = lane < hd
    rowq = _iota((qn, LANES), 0)
    lane_m = _mod(lane, hd)
    strict = rowq > lane_m
    incl = rowq >= lane_m
    r2 = _iota((2 * qn, LANES), 0)
    l2 = _iota((2 * qn, LANES), 1)
    bd = (r2 < qn) == (l2 < hd)
    strict_bd = bd & (_mod(r2, qn) > _mod(l2, hd))
    eye = (r2 == l2).astype(F32)
    steps = int(math.log2(qn))
    pairs = range(RWKV_HEADS // 2)
    sls = [slice(p * LANES, (p + 1) * LANES) for p in pairs]
    split = lambda x: jnp.concatenate([jnp.where(low, x, 0.0), jnp.where(low, 0.0, x)], axis=0)
    rs, ks, vs, bbs, ats, rts, cwls, eqs, lsts, rsts = ([] for _ in range(10))
    for sl in sls:
        lw = lw_all[:, sl]
        cw = cw_all[:, sl]
        r = _pad_rows(r_ref[0, :, sl], qn)
        k = _pad_rows(k_ref[0, :, sl], qn)
        v = _pad_rows(v_ref[0, :, sl], qn)
        kk = _pad_rows(kk_ref[0, :, sl], qn)
        av = _pad_rows(a_ref[0, :, sl], qn)
        cwl = cw[qn - 1:qn, :]
        e_neg = jnp.exp(-cw)
        bb = kk * av
        at = -kk * jnp.exp(cw - lw)
        rt = r * jnp.exp(cw)
        rs.append(r)
        ks.append(k)
        vs.append(v)
        bbs.append(bb)
        ats.append(at)
        rts.append(rt)
        cwls.append(cwl)
        eqs.append(jnp.exp(cwl - cw))
        lsts.append(jnp.concatenate([split(at), split(rt)], axis=0))
        rsts.append(jnp.concatenate([bb * e_neg, k * e_neg], axis=0))
    pms = [_bdot_nt(lsts[p], rsts[p]) for p in pairs]
    s_bds = [s_s[p] for p in pairs]
    ars = [_bdot_nt(jnp.concatenate([ats[p], rts[p]], axis=0), s_bds[p]) for p in pairs]
    n_bds, aaks, arks = [], [], []
    for pm in pms:
        p0, p1, p2, p3 = pm[0:qn], pm[qn:2 * qn], pm[2 * qn:3 * qn], pm[3 * qn:4 * qn]
        n_bds.append(jnp.where(strict_bd, jnp.concatenate([p0, pltpu.roll(p1, hd, 1)], axis=0), 0.0))
        aaks.append(jnp.where(strict, jnp.where(low, pltpu.roll(p0, hd, 1), p1), 0.0))
        arb = jnp.where(incl, jnp.where(low, p2, pltpu.roll(p3, hd, 1)), 0.0)
        ark = jnp.where(incl, jnp.where(low, pltpu.roll(p2, hd, 1), p3), 0.0)
        arks.append(jnp.concatenate([arb, ark], axis=1))
    mpows = [_bdot(n, n) for n in n_bds]
    t_invs = [eye + n for n in n_bds]
    for _ in range(steps - 2):
        prods = [_bdot(jnp.concatenate([t_invs[p], mpows[p]], axis=0), mpows[p]) for p in pairs]
        t_invs = [t_invs[p] + prods[p][0:2 * qn] for p in pairs]
        mpows = [prods[p][2 * qn:4 * qn] for p in pairs]
    t_invs = [t_invs[p] + _bdot(t_invs[p], mpows[p]) for p in pairs]
    v_ms = [split(v) for v in vs]
    rhss = [ars[p][0:qn] + _bdot(aaks[p], v_ms[p]) for p in pairs]
    u2s = [_bdot(t_invs[p], split(rhss[p])) for p in pairs]
    us = [u2[0:qn] + u2[qn:2 * qn] for u2 in u2s]
    ys = [ars[p][qn:2 * qn] + _bdot(arks[p], jnp.concatenate([split(us[p]), v_ms[p]], axis=0)) for p in pairs]
    upds = [_bdot_tn(jnp.concatenate([us[p], vs[p]], axis=0),
                     jnp.concatenate([bbs[p] * eqs[p], ks[p] * eqs[p]], axis=0)) for p in pairs]
    for p in pairs:
        s_s[p] = s_bds[p] * jnp.exp(cwls[p]) + jnp.where(bd, upds[p], 0.0)
    inv_n = 1.0 / hd
    lowb = low[0:lb]

    def half_sum(x):
        lo = jnp.sum(jnp.where(lowb, x, 0.0), axis=-1, keepdims=True)
        hi = jnp.sum(jnp.where(lowb, 0.0, x), axis=-1, keepdims=True)
        return jnp.where(lowb, lo, hi)

    for p, sl in enumerate(sls):
        y = ys[p][0:lb]
        mu = half_sum(y) * inv_n
        yc = y - mu
        var = half_sum(yc * yc) * inv_n
        yn = yc * lax.rsqrt(var + RWKV_GN_EPS) * gw_ref[:, sl] + gb_ref[:, sl]
        bonus = half_sum(rs[p][0:lb] * ks[p][0:lb] * rk_ref[:, sl]) * vs[p][0:lb]
        y_ref[0, :, sl] = (yn + bonus) * g_ref[0, :, sl]

    @pl.when(c == pl.num_programs(1) - 1)
    def _():
        so_ref[0] = s_s[...]


def _rwkv(r, lw, k, v, kk, a, g, r_k, gn_w, gn_b, s0_bd, *, lb):
    b, l, _ = r.shape
    nc = l // lb
    has_state = s0_bd is not None
    tok = pl.BlockSpec((1, lb, D_MODEL), lambda bi, c: (bi, c, 0))
    vec = pl.BlockSpec((1, D_MODEL), lambda bi, c: (0, 0))
    st = pl.BlockSpec((1, RWKV_HEADS // 2, LANES, LANES), lambda bi, c: (bi, 0, 0, 0))
    in_specs = [tok] * 7 + [vec] * 3 + ([st] if has_state else [])
    args = [r, lw, k, v, kk, a, g, r_k.reshape(1, -1), gn_w[None, :], gn_b[None, :]] + ([s0_bd] if has_state else [])
    return pl.pallas_call(
        functools.partial(_rwkv_kernel, lb=lb, has_state=has_state),
        grid=(b, nc),
        in_specs=in_specs,
        out_specs=[tok, st],
        out_shape=[jax.ShapeDtypeStruct((b, l, D_MODEL), F32),
                   jax.ShapeDtypeStruct((b, RWKV_HEADS // 2, LANES, LANES), F32)],
        scratch_shapes=[pltpu.VMEM((RWKV_HEADS // 2, LANES, LANES), F32)],
        compiler_params=_cp(("parallel", "arbitrary")),
        name="rwkv",
    )(*args)


def _memattn_kernel(q_ref, mk_ref, mv_ref, o_ref):
    scale = MEM_HEAD_DIM ** -0.5
    for h in range(MEM_HEADS):
        sl = slice(h * MEM_HEAD_DIM, (h + 1) * MEM_HEAD_DIM)
        s = _bdot_nt(q_ref[0, :, sl], mk_ref[0, 0, :, sl]) * scale
        m = jnp.max(s, axis=-1, keepdims=True)
        p = jnp.exp(s - m)
        l = jnp.sum(p, axis=-1, keepdims=True)
        o_ref[0, :, sl] = _bdot(p / l, mv_ref[0, 0, :, sl])


def _mem_attn(q, mk, mv, layer, *, tq):
    b, l, _ = q.shape
    tq = min(tq, l)
    mem_spec = pl.BlockSpec((1, 1, N_MEM, D_MODEL), lambda bi, i: (layer, bi, 0, 0))
    return pl.pallas_call(
        _memattn_kernel,
        grid=(b, l // tq),
        in_specs=[pl.BlockSpec((1, tq, D_MODEL), lambda bi, i: (bi, i, 0)), mem_spec, mem_spec],
        out_specs=pl.BlockSpec((1, tq, D_MODEL), lambda bi, i: (bi, i, 0)),
        out_shape=jax.ShapeDtypeStruct((b, l, D_MODEL), F32),
        compiler_params=_cp(("parallel", "arbitrary")),
        name="mem_attn",
    )(q, mk, mv)


def _pad_tokens(x, lp):
    b, l = x.shape[0], x.shape[1]
    if l == lp:
        return x
    return jnp.pad(x, ((0, 0), (0, lp - l)) + ((0, 0),) * (x.ndim - 2))


def _even_layer(x, w, e, grp):
    b, l = grp["b"], grp["l"]
    sample = grp["sample"]
    gain = w["norm_mix"][2 * e]
    pm = _mm([x], [w["e_w_main"][e]], gain=gain, name="e_in_main")
    v = _mm([x], [w["e_w_v"][e]], gain=gain, name="e_in_v")
    dt, dtt = _dt_proj(x, gain, w["e_w_dt"][e], w["e_dt_bias"][e])
    hg = SSD_HEADS // SSD_GROUPS
    lp = l if not sample else 8
    lq = max(lp, SSD_Q)
    dt4 = dt[:, :SSD_HEADS].reshape(b, l, SSD_GROUPS, hg).transpose(0, 2, 1, 3)
    dt4 = jnp.pad(dt4, ((0, 0), (0, 0), (0, lq - l), (0, LANES - hg)))
    dtt4 = dtt.reshape(SSD_GROUPS, hg, b, l).transpose(2, 0, 1, 3)
    dtt4 = jnp.pad(dtt4, ((0, 0), (0, 0), (0, 0), (0, lq - l)))
    pm_seq = pm.reshape(b, l, -1)
    pm3 = _pad_tokens(pm_seq, lp)
    kc1 = SSD_CONV - 1
    xbc_tail = pm_seq[:, max(l - kc1, 0):, SSD_INNER:SSD_INNER + XBC_DIM]
    if sample:
        conv_prev = grp["conv"].reshape(N_EVEN * b, kc1, XBC_DIM)
        s0 = grp["ssd"].reshape(N_EVEN * b, SSD_HEADS // 2, LANES, LANES)
        conv_new = jnp.concatenate([grp["conv"][e], xbc_tail], axis=1)[:, -kc1:]
    else:
        conv_prev, s0 = None, None
        assert l >= kc1
        conv_new = xbc_tail
    y_ssd, s_new = _ssd(pm3, dt4, dtt4, w["e_conv_w"][e], w["e_conv_b"][e], w["e_a_log"][e], w["e_d_skip"][e],
                        w["e_g_ssd"][e], conv_prev, s0, lb=min(lp, SSD_Q), state_off=e * b)
    y_ssd = y_ssd[:, :l].reshape(b * l, SSD_INNER)
    s_new = s_new.reshape(b, SSD_HEADS, SSD_HEAD_DIM, SSD_STATE)
    q0 = grp["q0"]
    cos_t, sin_t = _rope_tables(q0 + jnp.arange(l))
    if sample:
        cos_t, sin_t = jnp.tile(cos_t, (b, 1)), jnp.tile(sin_t, (b, 1))
        q_r, k_r = _qk_prep(pm, 5, 6, cos_t, sin_t, w["e_g_q"][e], w["e_g_k"][e], tq=b * l, with_kmean=False)
        th = lambda t: t.reshape(b, l * MOBA_HEADS, MOBA_HEAD_DIM)
        o = _moba_sample(grp["page_table"], th(q_r), th(k_r), th(v), grp["cache_k"], grp["cache_v"],
                         layer_off=e * grp["n_pool"]).reshape(b * l, D_MODEL)
    else:
        q_r, k_r, kmean = _qk_prep(pm, 5, 6, cos_t, sin_t, w["e_g_q"][e], w["e_g_k"][e], tq=MOBA_BLOCK,
                                   with_kmean=True)
        o = _moba_prompt(q_r, k_r, v, kmean.reshape(b, l // MOBA_BLOCK, D_MODEL), b=b, l=l)
    x = _mm([y_ssd, o], [w["e_w_out_a"][e], w["e_w_out_b"][e]], res=x, tm=512, name="e_out")
    hd = (b, l, MOBA_HEADS, MOBA_HEAD_DIM)
    return x, s_new, conv_new, k_r.reshape(hd), v.reshape(hd)


def _odd_layer(x, w, o, grp):
    b, l = grp["b"], grp["l"]
    sample = grp["sample"]
    gain = w["norm_mix"][2 * o + 1]
    cur = _mm([x], [w["o_w_cur"][o]], gain=gain, tn=1664, name="o_in_cur")
    pg = _mm([x], [w["o_w_g"][o]], gain=gain, name="o_in_g")
    gf = _mm([x], [w["o_w_gf"][o]], gain=gain, name="o_in_gf")
    cur3 = cur.reshape(b, l, RWKV_SHIFT_DIM)
    if sample:
        prev = jnp.concatenate([grp["shift"][o][:, None], cur3[:, :-1]], axis=1).reshape(b * l, RWKV_SHIFT_DIM)
    else:
        prev = None
    vecs = _rwkv_prep(cur, prev, w["o_mu"][o], w["o_w12"][o], w["o_w_g2"][o], w["o_w0"][o], w["o_a0"][o],
                      w["o_k_k"][o], w["o_k_a"][o], tm=512, seq_len=l)
    lp = l if not sample else 8
    vecs3 = [_pad_tokens(t.reshape(b, l, D_MODEL), lp) for t in vecs]
    if sample:
        s0 = grp["rwkv"][o].reshape(b, RWKV_HEADS // 2, 2, RWKV_HEAD_DIM, RWKV_HEAD_DIM)
        z = jnp.zeros_like(s0[:, :, 0])
        s0_bd = jnp.concatenate([jnp.concatenate([s0[:, :, 0], z], axis=-1),
                                 jnp.concatenate([z, s0[:, :, 1]], axis=-1)], axis=-2)
    else:
        s0_bd = None
    y_r, s_bd = _rwkv(*vecs3, w["o_r_k"][o], w["o_gn_w"][o], w["o_gn_b"][o], s0_bd, lb=min(lp, RWKV_Q))
    y_r = y_r[:, :l].reshape(b * l, D_MODEL)
    hd = RWKV_HEAD_DIM
    s_r = jnp.stack([s_bd[:, :, :hd, :hd], s_bd[:, :, hd:, hd:]], axis=2).reshape(b, RWKV_HEADS, hd, hd)
    pg3 = _pad_tokens(pg.reshape(b, l, -1), lp)
    gf3 = _pad_tokens(gf.reshape(b, l, LANES), lp)
    s0_g = grp["gla"].reshape(N_ODD * b, GLA_HEADS, GLA_DK, GLA_DV) if sample else None
    og, s_g = _gla(pg3, gf3, w["o_w_f2"][o], w["o_b_f"][o], w["o_g_gla"][o], s0_g,
                   lb=min(lp, GLA_Q), n_valid=min(l, GLA_Q), state_off=o * b)
    og = og[:, :l].reshape(b * l, GLA_HEADS * GLA_DV)
    x = _mm([y_r, og], [w["o_w_out_a"][o], w["o_w_out_b"][o]], res=x, tm=512, name="o_out")
    return x, s_r, cur3[:, -1], s_g


def _trunk(x, w, grp):
    b, l = grp["b"], grp["l"]
    outs = {k: [] for k in ("ssd", "conv", "k", "v", "rwkv", "shift", "gla")}
    for layer in range(DEPTH):
        if layer % 2 == 0:
            x, s, cnew, k, v = _even_layer(x, w, layer // 2, grp)
            outs["ssd"].append(s)
            outs["conv"].append(cnew)
            outs["k"].append(k)
            outs["v"].append(v)
        else:
            x, s_r, sh, s_g = _odd_layer(x, w, layer // 2, grp)
            outs["rwkv"].append(s_r)
            outs["shift"].append(sh)
            outs["gla"].append(s_g)
        qm = _mm([x], [w["m_w_q"][layer]], gain=w["norm_mem"][layer], head_gain=w["m_g_q"][layer], name="mem_q")
        lp = 8 if grp["sample"] else l
        om = _mem_attn(_pad_tokens(qm.reshape(b, l, D_MODEL), lp), grp["mem_k"], grp["mem_v"], layer, tq=512)
        x = _mm([om[:, :l].reshape(b * l, D_MODEL)], [w["m_w_o"][layer]], res=x, name="mem_o")
        x = _ffn(x, w["norm_ffn"][layer], w["f_w_g"][layer], w["f_w_u"][layer], w["f_w_d"][layer])
    return x, {k: jnp.stack(v) for k, v in outs.items()}


def kernel(x_prompt, x_sample, cache_moba_k, cache_moba_v, state_ssd, state_ssd_conv, state_rwkv, state_rwkv_shift, state_gla, cache_mem_k, cache_mem_v, page_table, mem_prompt, norm_mix, norm_mem, norm_memtok, norm_ffn, e_w_in, e_conv_w, e_conv_b, e_dt_bias, e_a_log, e_d_skip, e_g_ssd, e_g_q, e_g_k, e_w_out, o_w_in, o_mu, o_w0, o_w_w2, o_a0, o_w_a2, o_w_g2, o_k_k, o_k_a, o_r_k, o_gn_w, o_gn_b, o_w_f2, o_b_f, o_g_gla, o_w_out, m_w_q, m_w_kv, m_g_q, m_g_k, m_w_o, f_w_gu, f_w_down):
    bp, lp, _ = x_prompt.shape
    bs, ls, _ = x_sample.shape
    n_pool = cache_moba_k.shape[1]
    n_pages = page_table.shape[1]
    c_dt = SSD_INNER + XBC_DIM
    c_q = c_dt + SSD_HEADS
    c_cur = RWKV_SHIFT_DIM
    z64 = jnp.zeros((N_ODD, 64, D_MODEL), F32)
    w = dict(
        norm_mix=norm_mix, norm_mem=norm_mem, norm_ffn=norm_ffn,
        e_w_main=jnp.concatenate([e_w_in[:, :, :c_dt], e_w_in[:, :, c_q:c_q + 2 * D_MODEL]], axis=-1).astype(BF16),
        e_w_v=e_w_in[:, :, c_q + 2 * D_MODEL:].astype(BF16),
        e_w_dt=e_w_in[:, :, c_dt:c_q],
        e_conv_w=e_conv_w, e_conv_b=e_conv_b, e_dt_bias=e_dt_bias, e_a_log=e_a_log, e_d_skip=e_d_skip,
        e_g_ssd=e_g_ssd, e_g_q=e_g_q, e_g_k=e_g_k,
        e_w_out_a=e_w_out[:, :SSD_INNER].astype(BF16), e_w_out_b=e_w_out[:, SSD_INNER:].astype(BF16),
        o_w_cur=o_w_in[:, :, :c_cur].astype(BF16),
        o_w_g=jnp.concatenate([o_w_in[:, :, c_cur + 2 * GLA_KEY_DIM:c_cur + 2 * GLA_KEY_DIM + 2 * D_MODEL],
                               o_w_in[:, :, c_cur:c_cur + 2 * GLA_KEY_DIM]], axis=-1).astype(BF16),
        o_w_gf=jnp.pad(o_w_in[:, :, c_cur + 2 * GLA_KEY_DIM + 2 * D_MODEL:],
                       ((0, 0), (0, 0), (0, LANES - GLA_GATE_RANK))).astype(BF16),
        o_mu=o_mu, o_w0=o_w0, o_a0=o_a0, o_k_k=o_k_k, o_k_a=o_k_a, o_r_k=o_r_k, o_gn_w=o_gn_w, o_gn_b=o_gn_b,
        o_w12=jnp.concatenate([jnp.concatenate([o_w_w2, z64], axis=-1),
                               jnp.concatenate([z64, o_w_a2], axis=-1)], axis=1).astype(BF16),
        o_w_g2=o_w_g2.astype(BF16), o_w_f2=o_w_f2, o_b_f=o_b_f, o_g_gla=o_g_gla,
        o_w_out_a=o_w_out[:, :D_MODEL].astype(BF16), o_w_out_b=o_w_out[:, D_MODEL:].astype(BF16),
        m_w_q=m_w_q.astype(BF16), m_g_q=m_g_q, m_w_o=m_w_o.astype(BF16),
        f_w_g=f_w_gu[:, :, :D_FF].astype(BF16), f_w_u=f_w_gu[:, :, D_FF:].astype(BF16),
        f_w_d=f_w_down.astype(BF16),
    )
    mem2 = mem_prompt.reshape(bp * N_MEM, D_MODEL)
    mks, mvs = [], []
    for layer in range(DEPTH):
        wkv = m_w_kv[layer].astype(BF16)
        mks.append(_mm([mem2], [wkv[:, :D_MODEL]], gain=norm_memtok[layer], head_gain=m_g_k[layer], name="mem_k"))
        mvs.append(_mm([mem2], [wkv[:, D_MODEL:]], gain=norm_memtok[layer], name="mem_v"))
    mem_k_p = jnp.stack(mks).reshape(DEPTH, bp, N_MEM, D_MODEL)
    mem_v_p = jnp.stack(mvs).reshape(DEPTH, bp, N_MEM, D_MODEL)

    grp_p = dict(b=bp, l=lp, sample=False, q0=0, mem_k=mem_k_p, mem_v=mem_v_p)
    y_p, o_p = _trunk(x_prompt.reshape(bp * lp, D_MODEL), w, grp_p)

    grp_s = dict(b=bs, l=ls, sample=True, q0=n_pages * PAGE_SIZE, page_table=page_table, n_pool=n_pool,
                 cache_k=cache_moba_k.reshape(N_EVEN * n_pool, PAGE_SIZE * MOBA_HEADS, MOBA_HEAD_DIM),
                 cache_v=cache_moba_v.reshape(N_EVEN * n_pool, PAGE_SIZE * MOBA_HEADS, MOBA_HEAD_DIM),
                 ssd=state_ssd, conv=state_ssd_conv, rwkv=state_rwkv, shift=state_rwkv_shift, gla=state_gla,
                 mem_k=cache_mem_k.reshape(DEPTH, bs, N_MEM, D_MODEL),
                 mem_v=cache_mem_v.reshape(DEPTH, bs, N_MEM, D_MODEL))
    y_s, o_s = _trunk(x_sample.reshape(bs * ls, D_MODEL), w, grp_s)

    mem_shape = (DEPTH, bp, N_MEM, MEM_HEADS, MEM_HEAD_DIM)
    return (y_p.reshape(bp, lp, D_MODEL), y_s.reshape(bs, ls, D_MODEL),
            o_p["k"], o_p["v"], o_s["k"], o_s["v"], o_p["ssd"], o_s["ssd"], o_p["conv"], o_s["conv"],
            o_p["rwkv"], o_s["rwkv"], o_p["shift"], o_s["shift"], o_p["gla"], o_s["gla"],
            mem_k_p.reshape(mem_shape), mem_v_p.reshape(mem_shape))
```

```python
import functools
import math

import jax
import jax.numpy as jnp
from jax import lax
from jax.experimental import pallas as pl
from jax.experimental.pallas import tpu as pltpu

F32 = jnp.float32
BF16 = jnp.bfloat16

D_MODEL = 1024
DEPTH = 4
N_EVEN = 2
N_ODD = 2
NORM_EPS = 1e-6
PAGE_SIZE = 128

SSD_INNER = 2048
SSD_HEAD_DIM = 64
SSD_HEADS = 32
SSD_GROUPS = 4
SSD_STATE = 128
SSD_CONV = 4
XBC_DIM = SSD_INNER + 2 * SSD_GROUPS * SSD_STATE
SSD_Q = 128

MOBA_HEADS = 8
MOBA_HEAD_DIM = 128
MOBA_BLOCK = 256
MOBA_TOPK = 3
ROT_DIM = 32
ROPE_THETA = 500000.0

RWKV_HEAD_DIM = 64
RWKV_HEADS = 16
RWKV_GN_EPS = 64e-5
RWKV_SHIFT_DIM = 3 * D_MODEL + 64 + 64 + 128
RWKV_Q = 64

GLA_HEADS = 4
GLA_DK = 128
GLA_DV = 256
GLA_KEY_DIM = 512
GLA_GATE_RANK = 16
GLA_GATE_NORM = 16.0
GLA_Q = 64

N_MEM = 256
MEM_HEADS = 4
MEM_HEAD_DIM = 256
D_FF = 2816

LANES = 128
SUBLANES = 8
VMEM_LIMIT = 56 * 1024 * 1024


def _cp(sem, vmem=VMEM_LIMIT):
    return pltpu.CompilerParams(dimension_semantics=sem, vmem_limit_bytes=vmem)


def _bdot(a, b):
    return jnp.dot(a.astype(BF16), b.astype(BF16), preferred_element_type=F32)


def _bdot_nt(a, b):
    return lax.dot_general(a.astype(BF16), b.astype(BF16), (((1,), (1,)), ((), ())),
                           preferred_element_type=F32)


def _bdot_tn(a, b):
    k = a.shape[0]
    kp = -(-k // LANES) * LANES
    if kp != k:
        a = jnp.concatenate([a, jnp.zeros((kp - k, a.shape[1]), a.dtype)], axis=0)
        b = jnp.concatenate([b, jnp.zeros((kp - k, b.shape[1]), b.dtype)], axis=0)
    return jnp.dot(a.T.astype(BF16), b.astype(BF16), preferred_element_type=F32)


def _split3(x):
    hi = x.astype(BF16)
    r1 = x - hi.astype(F32)
    mid = r1.astype(BF16)
    lo = (r1 - mid.astype(F32)).astype(BF16)
    return hi, mid, lo


def _exact_left(m01, x):
    hi, mid, lo = _split3(x)
    m = m01.astype(BF16)
    return (jnp.dot(m, hi, preferred_element_type=F32) + jnp.dot(m, mid, preferred_element_type=F32)
            + jnp.dot(m, lo, preferred_element_type=F32))


def _exact_right(x, m01):
    hi, mid, lo = _split3(x)
    m = m01.astype(BF16)
    return (jnp.dot(hi, m, preferred_element_type=F32) + jnp.dot(mid, m, preferred_element_type=F32)
            + jnp.dot(lo, m, preferred_element_type=F32))


def _f32_nt(a, b):
    ah, am, al = _split3(a)
    bh, bm, bl = _split3(b)
    dn = (((1,), (1,)), ((), ()))
    d = lambda x, y: lax.dot_general(x, y, dn, preferred_element_type=F32)
    return d(ah, bh) + (d(ah, bm) + d(am, bh)) + (d(am, bm) + d(ah, bl) + d(al, bh))


def _rms(x, g, eps=NORM_EPS):
    return x * lax.rsqrt(jnp.mean(x * x, axis=-1, keepdims=True) + eps) * g


def _sigmoid(x):
    return 1.0 / (1.0 + jnp.exp(-x))


def _silu(x):
    return x * _sigmoid(x)


def _softplus(x):
    return jnp.maximum(x, 0.0) + jnp.log1p(jnp.exp(-jnp.abs(x)))


def _iota(shape, dim):
    return lax.broadcasted_iota(jnp.int32, shape, dim)


def _div(x, d):
    return lax.shift_right_logical(x, jnp.int32(int(math.log2(d))))


def _mod(x, d):
    return x & jnp.int32(d - 1)


def _pad_rows(x, rows):
    if x.shape[0] == rows:
        return x
    return jnp.concatenate([x, jnp.zeros((rows - x.shape[0], x.shape[1]), x.dtype)], axis=0)


def _seg_ones(seg):
    r = _div(_iota((LANES, LANES), 0), seg)
    c = _div(_iota((LANES, LANES), 1), seg)
    return (r == c).astype(BF16)


def _seg_sum(x, ones_bd):
    parts = []
    for s in range(x.shape[1] // LANES):
        parts.append(_exact_right(x[:, s * LANES:(s + 1) * LANES], ones_bd))
    return parts[0] if len(parts) == 1 else jnp.concatenate(parts, axis=1)


def _mm_kernel(*refs, n_a, has_gain, hn_dim, has_res):
    pos = 0
    a_refs = refs[pos:pos + n_a]
    pos += n_a
    g_ref = None
    if has_gain:
        g_ref = refs[pos]
        pos += 1
    w_refs = refs[pos:pos + n_a]
    pos += n_a
    hn_ref = None
    if hn_dim:
        hn_ref = refs[pos]
        pos += 1
    res_ref = None
    if has_res:
        res_ref = refs[pos]
        pos += 1
    o_ref = refs[pos]
    pos += 1
    if has_gain:
        an_ref = refs[pos]

        @pl.when(pl.program_id(1) == 0)
        def _():
            an_ref[...] = _rms(a_refs[0][...].astype(F32), g_ref[...]).astype(BF16)

        acc = jnp.dot(an_ref[...], w_refs[0][...], preferred_element_type=F32)
    else:
        acc = None
        for a_ref, w_ref in zip(a_refs, w_refs):
            d = jnp.dot(a_ref[...].astype(BF16), w_ref[...], preferred_element_type=F32)
            acc = d if acc is None else acc + d
    if hn_dim:
        parts = [_rms(acc[:, s * hn_dim:(s + 1) * hn_dim], hn_ref[...]) for s in range(acc.shape[1] // hn_dim)]
        acc = parts[0] if len(parts) == 1 else jnp.concatenate(parts, axis=1)
    if has_res:
        acc = acc + res_ref[...]
    o_ref[...] = acc.astype(o_ref.dtype)


def _mm(a_list, w_list, *, gain=None, head_gain=None, res=None, tm=1024, tn=1024, out_dtype=F32, name="mm"):
    m = a_list[0].shape[0]
    n = w_list[0].shape[1]
    tm = min(tm, m)
    tn = min(tn, n)
    assert m % tm == 0 and n % tn == 0, (m, tm, n, tn)
    has_gain = gain is not None
    assert not has_gain or len(a_list) == 1
    in_specs = [pl.BlockSpec((tm, a.shape[1]), lambda i, j: (i, 0)) for a in a_list]
    args = list(a_list)
    if has_gain:
        in_specs.append(pl.BlockSpec((1, a_list[0].shape[1]), lambda i, j: (0, 0)))
        args.append(gain.reshape(1, -1))
    for w in w_list:
        in_specs.append(pl.BlockSpec((w.shape[0], tn), lambda i, j: (0, j)))
        args.append(w)
    hn_dim = 0
    if head_gain is not None:
        hn_dim = head_gain.shape[-1]
        in_specs.append(pl.BlockSpec((1, hn_dim), lambda i, j: (0, 0)))
        args.append(head_gain.reshape(1, -1))
    if res is not None:
        in_specs.append(pl.BlockSpec((tm, tn), lambda i, j: (i, j)))
        args.append(res)
    scratch = [pltpu.VMEM((tm, a_list[0].shape[1]), BF16)] if has_gain else []
    return pl.pallas_call(
        functools.partial(_mm_kernel, n_a=len(a_list), has_gain=has_gain, hn_dim=hn_dim, has_res=res is not None),
        grid=(m // tm, n // tn),
        in_specs=in_specs,
        out_specs=pl.BlockSpec((tm, tn), lambda i, j: (i, j)),
        out_shape=jax.ShapeDtypeStruct((m, n), out_dtype),
        scratch_shapes=scratch,
        compiler_params=_cp(("parallel", "arbitrary")),
        name=name,
    )(*args)


def _ffn_kernel(x_ref, g_ref, wg_ref, wu_ref, wd_ref, o_ref, an_ref, acc_ref):
    f = pl.program_id(1)

    @pl.when(f == 0)
    def _():
        an_ref[...] = _rms(x_ref[...], g_ref[...]).astype(BF16)
        acc_ref[...] = jnp.zeros_like(acc_ref)

    a = an_ref[...]
    g = jnp.dot(a, wg_ref[...], preferred_element_type=F32)
    u = jnp.dot(a, wu_ref[...], preferred_element_type=F32)
    act = (_silu(g) * u).astype(BF16)
    acc_ref[...] += jnp.dot(act, wd_ref[...], preferred_element_type=F32)

    @pl.when(f == pl.num_programs(1) - 1)
    def _():
        o_ref[...] = x_ref[...] + acc_ref[...]


def _ffn(x, gain, wg, wu, wd, *, tm=1024, tf=256):
    m = x.shape[0]
    tm = min(tm, m)
    nf = D_FF // tf
    return pl.pallas_call(
        _ffn_kernel,
        grid=(m // tm, nf),
        in_specs=[pl.BlockSpec((tm, D_MODEL), lambda i, f: (i, 0)),
                  pl.BlockSpec((1, D_MODEL), lambda i, f: (0, 0)),
                  pl.BlockSpec((D_MODEL, tf), lambda i, f: (0, f)),
                  pl.BlockSpec((D_MODEL, tf), lambda i, f: (0, f)),
                  pl.BlockSpec((tf, D_MODEL), lambda i, f: (f, 0))],
        out_specs=pl.BlockSpec((tm, D_MODEL), lambda i, f: (i, 0)),
        out_shape=jax.ShapeDtypeStruct((m, D_MODEL), F32),
        scratch_shapes=[pltpu.VMEM((tm, D_MODEL), BF16), pltpu.VMEM((tm, D_MODEL), F32)],
        compiler_params=_cp(("parallel", "arbitrary")),
        name="ffn",
    )(x, gain.reshape(1, -1), wg, wu, wd)


def _qkprep_kernel(q_ref, k_ref, cos_ref, sin_ref, gq_ref, gk_ref, qo_ref, ko_ref, *maybe_km, with_kmean):
    cos = cos_ref[...]
    sin = sin_ref[...]
    lane = _iota(cos.shape, 1)
    half = ROT_DIM // 2

    def prep(x_ref, g_ref, o_ref):
        for h in range(MOBA_HEADS):
            sl = slice(h * MOBA_HEAD_DIM, (h + 1) * MOBA_HEAD_DIM)
            y = _rms(x_ref[:, sl], g_ref[...])
            sw = jnp.where(lane < half, pltpu.roll(y, MOBA_HEAD_DIM - half, 1), pltpu.roll(y, half, 1))
            o_ref[:, sl] = y * cos + sw * sin

    prep(q_ref, gq_ref, qo_ref)
    prep(k_ref, gk_ref, ko_ref)
    if with_kmean:
        maybe_km[0][0] = jnp.mean(ko_ref[...], axis=0, keepdims=True)


def _rope_tables(pos):
    half = ROT_DIM // 2
    inv_freq = ROPE_THETA ** (-jnp.arange(half, dtype=F32) / half)
    ang = pos.astype(F32)[:, None] * inv_freq[None, :]
    cos, sin = jnp.cos(ang), jnp.sin(ang)
    ones = jnp.ones((pos.shape[0], MOBA_HEAD_DIM - ROT_DIM), F32)
    cos_t = jnp.concatenate([cos, cos, ones], axis=1)
    sin_t = jnp.concatenate([-sin, sin, 0.0 * ones], axis=1)
    return cos_t, sin_t


def _qk_prep(proj, q_col, k_col, cos_t, sin_t, gq, gk, *, tq, with_kmean):
    t = proj.shape[0]
    n_tab = cos_t.shape[0] // tq
    out_shape = [jax.ShapeDtypeStruct((t, D_MODEL), F32), jax.ShapeDtypeStruct((t, D_MODEL), F32)]
    out_specs = [pl.BlockSpec((tq, D_MODEL), lambda i: (i, 0)), pl.BlockSpec((tq, D_MODEL), lambda i: (i, 0))]
    if with_kmean:
        out_shape.append(jax.ShapeDtypeStruct((t // tq, 1, D_MODEL), F32))
        out_specs.append(pl.BlockSpec((1, 1, D_MODEL), lambda i: (i, 0, 0)))
    return pl.pallas_call(
        functools.partial(_qkprep_kernel, with_kmean=with_kmean),
        grid=(t // tq,),
        in_specs=[pl.BlockSpec((tq, D_MODEL), lambda i: (i, q_col)),
                  pl.BlockSpec((tq, D_MODEL), lambda i: (i, k_col)),
                  pl.BlockSpec((tq, MOBA_HEAD_DIM), lambda i: (i % n_tab, 0)),
                  pl.BlockSpec((tq, MOBA_HEAD_DIM), lambda i: (i % n_tab, 0)),
                  pl.BlockSpec((1, MOBA_HEAD_DIM), lambda i: (0, 0)),
                  pl.BlockSpec((1, MOBA_HEAD_DIM), lambda i: (0, 0))],
        out_specs=out_specs,
        out_shape=out_shape,
        compiler_params=_cp(("parallel",)),
        name="qk_prep",
    )(proj, proj, cos_t, sin_t, gq.reshape(1, -1), gk.reshape(1, -1))


def _moba_prompt_kernel(q_ref, k_ref, v_ref, km_ref, o_ref, *, nb, n_sel):
    qi = pl.program_id(2)
    blk = MOBA_BLOCK
    scale = MOBA_HEAD_DIM ** -0.5
    q = q_ref[...]
    qb = (q * scale).astype(BF16)
    gt = _f32_nt(_pad_rows(km_ref[0], LANES), q)[0:SUBLANES]
    rown = _iota((SUBLANES, blk), 0)
    past = rown < qi
    sel_t = jnp.zeros((SUBLANES, blk), F32)
    for n in range(nb):
        g_n = gt[n:n + 1, :]
        beats = past & ((gt > g_n) | ((gt == g_n) & (rown < n)))
        sel_n = (jnp.sum(beats.astype(F32), axis=0, keepdims=True) < n_sel).astype(F32)
        sel_t = jnp.where(rown == n, sel_n, sel_t)
    sel_m = _pad_rows(sel_t, LANES).T
    ci = _iota((blk, blk), 1)
    rw = _iota((blk, LANES), 0)
    cn = _iota((blk, LANES), 1)
    thr = jnp.where(cn == qi, rw, jnp.where((sel_m > 0.5) & (cn < qi), blk - 1, -1))

    def attend(nblocks):
        s = _bdot_nt(qb, k_ref[0:nblocks * blk, :])
        parts = []
        for n in range(nblocks):
            parts.append(jnp.where(ci <= thr[:, n:n + 1], s[:, n * blk:(n + 1) * blk], -jnp.inf))
        s = parts[0] if nblocks == 1 else jnp.concatenate(parts, axis=1)
        m = jnp.max(s, axis=-1, keepdims=True)
        p = jnp.exp(s - m)
        l = jnp.sum(p, axis=-1, keepdims=True)
        o_ref[...] = _bdot(p, v_ref[0:nblocks * blk, :]) / l

    stride = max(nb // 4, 1)
    sizes = list(range(stride, nb, stride)) + [nb]
    lo = 0
    for size in sizes:
        pl.when((qi >= lo) & (qi < size))(functools.partial(attend, size))
        lo = size


def _moba_prompt(q, k, v, kmean, *, b, l):
    nb = l // MOBA_BLOCK
    assert 1 <= nb <= SUBLANES
    n_sel = min(MOBA_TOPK, (l - 1) // MOBA_BLOCK)
    hd = MOBA_HEAD_DIM
    return pl.pallas_call(
        functools.partial(_moba_prompt_kernel, nb=nb, n_sel=n_sel),
        grid=(b, MOBA_HEADS, nb),
        in_specs=[pl.BlockSpec((MOBA_BLOCK, hd), lambda bi, h, qi: (bi * nb + qi, h)),
                  pl.BlockSpec((l, hd), lambda bi, h, qi: (bi, h)),
                  pl.BlockSpec((l, hd), lambda bi, h, qi: (bi, h)),
                  pl.BlockSpec((1, nb, hd), lambda bi, h, qi: (bi, 0, h))],
        out_specs=pl.BlockSpec((MOBA_BLOCK, hd), lambda bi, h, qi: (bi * nb + qi, h)),
        out_shape=jax.ShapeDtypeStruct((b * l, D_MODEL), F32),
        compiler_params=_cp(("parallel", "parallel", "arbitrary")),
        name="moba_prompt",
    )(q, k, v, kmean)


def _moba_sample_kernel(pt_ref, q_ref, kn_ref, vn_ref, *rest, n_pages, pps, l_real, n_sel):
    del pt_ref
    kc_refs, vc_refs = rest[0:pps], rest[pps:2 * pps]
    o_ref, m_s, l_s, acc_s, ks_s = rest[2 * pps:]
    step = pl.program_id(1)
    nh = MOBA_HEADS
    rows = l_real * nh
    scale = MOBA_HEAD_DIM ** -0.5
    pages_per_block = MOBA_BLOCK // PAGE_SIZE
    q = q_ref[0]
    width = PAGE_SIZE * nh
    same_head = _mod(_iota((rows, width), 0), nh) == _mod(_iota((rows, width), 1), nh)
    kps = [r[0] for r in kc_refs]
    ss = [jnp.where(same_head, _bdot_nt(q, kp) * scale, -jnp.inf) for kp in kps]
    ms = [jnp.max(s, axis=-1, keepdims=True) for s in ss]
    prs = [jnp.exp(s - m) for s, m in zip(ss, ms)]
    accs = [_bdot(pr, r[0]) for pr, r in zip(prs, vc_refs)]
    ksums = [jnp.sum(kp.reshape(PAGE_SIZE, nh, MOBA_HEAD_DIM), axis=0) for kp in kps]
    for i in range(pps):
        pg = step * pps + i
        m_s[pg] = ms[i]
        l_s[pg] = jnp.sum(prs[i], axis=-1, keepdims=True)
        acc_s[pg] = accs[i]
    for j in range(pps // pages_per_block):
        tot = ksums[j * pages_per_block]
        for i in range(1, pages_per_block):
            tot = tot + ksums[j * pages_per_block + i]
        ks_s[step * (pps // pages_per_block) + j] = tot

    @pl.when(step == n_pages // pps - 1)
    def _():
        n_blocks = n_pages // pages_per_block
        gates = []
        for n in range(n_blocks):
            kmean = ks_s[n] * (1.0 / MOBA_BLOCK)
            gates.append(jnp.sum(q * jnp.concatenate([kmean] * l_real, axis=0), axis=-1, keepdims=True))
        sels = []
        for n in range(n_blocks):
            rank = jnp.zeros((rows, 1), F32)
            for j in range(n_blocks):
                if j != n:
                    beats = (gates[j] > gates[n]) | ((gates[j] == gates[n]) & (j < n))
                    rank = rank + beats.astype(F32)
            sels.append(rank < n_sel)
        kn = _pad_rows(kn_ref[0], LANES)
        vn = _pad_rows(vn_ref[0], LANES)
        ro = _iota((rows, LANES), 0)
        co = _iota((rows, LANES), 1)
        vis = (_mod(ro, nh) == _mod(co, nh)) & (_div(co, nh) <= _div(ro, nh))
        s_own = jnp.where(vis, _bdot_nt(q, kn) * scale, -jnp.inf)
        big = jnp.max(s_own, axis=-1, keepdims=True)
        for pg in range(n_pages):
            big = jnp.maximum(big, jnp.where(sels[pg // pages_per_block], m_s[pg], -jnp.inf))
        p_own = jnp.exp(s_own - big)
        den = jnp.sum(p_own, axis=-1, keepdims=True)
        num = _bdot(p_own, vn)
        for pg in range(n_pages):
            w = jnp.where(sels[pg // pages_per_block], jnp.exp(m_s[pg] - big), 0.0)
            den = den + w * l_s[pg]
            num = num + w * acc_s[pg]
        o_ref[0] = num / den


def _moba_sample(page_table, q, k_new, v_new, cache_k, cache_v, *, layer_off):
    b, n_pages = page_table.shape
    rows = q.shape[1]
    l_real = rows // MOBA_HEADS
    q0 = n_pages * PAGE_SIZE
    assert q0 % MOBA_BLOCK == 0 and rows <= LANES
    n_sel = min(MOBA_TOPK, (q0 + l_real - 1) // MOBA_BLOCK)
    hd = MOBA_HEAD_DIM
    pps = n_pages
    assert n_pages % pps == 0 and pps % (MOBA_BLOCK // PAGE_SIZE) == 0
    tok_spec = pl.BlockSpec((1, rows, hd), lambda bi, p, pt: (bi, 0, 0))

    def page_spec(i):
        return pl.BlockSpec((1, PAGE_SIZE * MOBA_HEADS, hd),
                            lambda bi, p, pt: (layer_off + pt[bi, p * pps + i], 0, 0))

    grid_spec = pltpu.PrefetchScalarGridSpec(
        num_scalar_prefetch=1,
        grid=(b, n_pages // pps),
        in_specs=([tok_spec, tok_spec, tok_spec] + [page_spec(i) for i in range(pps)]
                  + [page_spec(i) for i in range(pps)]),
        out_specs=tok_spec,
        scratch_shapes=[pltpu.VMEM((n_pages, rows, 1), F32),
                        pltpu.VMEM((n_pages, rows, 1), F32),
                        pltpu.VMEM((n_pages, rows, hd), F32),
                        pltpu.VMEM((n_pages * PAGE_SIZE // MOBA_BLOCK, MOBA_HEADS, hd), F32)],
    )
    return pl.pallas_call(
        functools.partial(_moba_sample_kernel, n_pages=n_pages, pps=pps, l_real=l_real, n_sel=n_sel),
        grid_spec=grid_spec,
        out_shape=jax.ShapeDtypeStruct((b, rows, hd), F32),
        compiler_params=_cp(("parallel", "arbitrary")),
        name="moba_sample",
    )(page_table, q, k_new, v_new, *([cache_k] * pps), *([cache_v] * pps))


def _ssd_kernel(*refs, lb, has_state):
    if has_state:
        (z_ref, x_ref, b_ref, c_ref, dt_ref, dtt_ref, arow_ref, acol_ref, cwx_ref, cwb_ref, cwc_ref,
         cbx_ref, cbb_ref, cbc_ref, dl_ref, gs_ref, px_ref, pb_ref, pc_ref, s0_ref,
         y_ref, so_ref, s_s, tx_s, tb_s, tc_s) = refs
    else:
        (z_ref, x_ref, b_ref, c_ref, dt_ref, dtt_ref, arow_ref, acol_ref, cwx_ref, cwb_ref, cwc_ref,
         cbx_ref, cbb_ref, cbc_ref, dl_ref, gs_ref,
         y_ref, so_ref, s_s, tx_s, tb_s, tc_s) = refs
    c = pl.program_id(1)
    q = SSD_Q
    kc = SSD_CONV

    @pl.when(c == 0)
    def _():
        if has_state:
            s_s[...] = s0_ref[0]
            for t_s, p_ref in ((tx_s, px_ref), (tb_s, pb_ref), (tc_s, pc_ref)):
                t_s[...] = jnp.zeros_like(t_s)
                t_s[8 - (kc - 1):8, :] = p_ref[0]
        else:
            s_s[...] = jnp.zeros_like(s_s)
            tx_s[...] = jnp.zeros_like(tx_s)
            tb_s[...] = jnp.zeros_like(tb_s)
            tc_s[...] = jnp.zeros_like(tc_s)

    def conv(x, tail_s, w_ref, bias_ref):
        prev8 = tail_s[...]
        row8 = _iota((8, x.shape[1]), 0)
        out = x * w_ref[kc - 1:kc, :]
        for j in range(1, kc):
            xs = pltpu.roll(x, j, 0)
            head = jnp.where(row8 < j, pltpu.roll(prev8, j, 0), xs[0:8])
            sh = head if lb == 8 else jnp.concatenate([head, xs[8:]], axis=0)
            out = out + sh * w_ref[kc - 1 - j:kc - j, :]
        tail_s[...] = x[lb - 8:lb]
        return _silu(out + bias_ref[...])

    xc = _pad_rows(conv(x_ref[0], tx_s, cwx_ref, cbx_ref), q)
    bc_all = _pad_rows(conv(b_ref[0], tb_s, cwb_ref, cbb_ref), q)
    cc_all = _pad_rows(conv(c_ref[0], tc_s, cwc_ref, cbc_ref), q)
    ri = _iota((q, q), 0)
    ci = _iota((q, q), 1)
    causal = ri >= ci
    upper = ri <= ci
    lane = _iota((q, LANES), 1)
    low = lane < SSD_HEAD_DIM
    row128 = _iota((2 * SSD_HEAD_DIM, 1), 0)
    grps = range(SSD_GROUPS)
    prs = [(g, pp) for g in grps for pp in range(4)]
    dts = [dt_ref[0, g] for g in grps]
    dtts = [dtt_ref[0, g] for g in grps]
    css = [_exact_left(causal, dts[g] * -jnp.exp(arow_ref[g])) for g in grps]
    csts = [_exact_right(dtts[g] * -jnp.exp(acol_ref[g]), upper) for g in grps]
    bcs = [bc_all[:, g * LANES:(g + 1) * LANES] for g in grps]
    ccs = [cc_all[0:lb, g * LANES:(g + 1) * LANES] for g in grps]
    cbs = [_bdot_nt(ccs[g], bcs[g]) for g in grps]
    xps = [xc[:, (4 * g + pp) * LANES:(4 * g + pp + 1) * LANES] for g, pp in prs]
    sps = [s_s[4 * g + pp] for g, pp in prs]
    inter = [_bdot_nt(ccs[g], sps[i]) for i, (g, pp) in enumerate(prs)]

    def w_head(g, h):
        dec = jnp.where(causal[0:lb], jnp.exp(css[g][0:lb, h:h + 1] - csts[g][h:h + 1, :]), 0.0)
        return cbs[g] * dec * dtts[g][h:h + 1, :]

    intra = [_bdot(w_head(g, 2 * pp), jnp.where(low, xps[i], 0.0))
             + _bdot(w_head(g, 2 * pp + 1), jnp.where(low, 0.0, xps[i])) for i, (g, pp) in enumerate(prs)]
    csa = [css[g][:, 2 * pp:2 * pp + 1] for g, pp in prs]
    csb = [css[g][:, 2 * pp + 1:2 * pp + 2] for g, pp in prs]
    tails = [jnp.where(low, jnp.exp(csa[i][q - 1:q, :] - csa[i]) * dts[g][:, 2 * pp:2 * pp + 1],
                       jnp.exp(csb[i][q - 1:q, :] - csb[i]) * dts[g][:, 2 * pp + 1:2 * pp + 2])
             for i, (g, pp) in enumerate(prs)]
    upds = [_bdot_tn(xps[i] * tails[i], bcs[g]) for i, (g, pp) in enumerate(prs)]
    ys = []
    for i, (g, pp) in enumerate(prs):
        s_s[i] = sps[i] * jnp.where(row128 < SSD_HEAD_DIM, jnp.exp(csa[i][q - 1:q, :]),
                                    jnp.exp(csb[i][q - 1:q, :])) + upds[i]
        ys.append(intra[i] + inter[i] * jnp.where(low[0:lb], jnp.exp(csa[i][0:lb]), jnp.exp(csb[i][0:lb]))
                  + dl_ref[:, i * LANES:(i + 1) * LANES] * xps[i][0:lb])
    gw = SSD_INNER // SSD_GROUPS
    for g in grps:
        y = jnp.concatenate(ys[4 * g:4 * g + 4], axis=1) * _silu(z_ref[0, :, g * gw:(g + 1) * gw])
        y_ref[0, :, g * gw:(g + 1) * gw] = _rms(y, gs_ref[:, g * gw:(g + 1) * gw])

    @pl.when(c == pl.num_programs(1) - 1)
    def _():
        so_ref[0] = s_s[...]


def _ssd(pm, dt, dtt, conv_w, conv_b, a_log, d_skip, g_ssd, conv_prev, s0, *, lb, state_off=0):
    b, l, _ = pm.shape
    nc = l // lb
    hg = SSD_HEADS // SSD_GROUPS
    gw = SSD_INNER // SSD_GROUPS
    has_state = s0 is not None
    cwx, cwb, cwc = conv_w[:, :SSD_INNER], conv_w[:, SSD_INNER:SSD_INNER + 512], conv_w[:, SSD_INNER + 512:]
    cbx, cbb, cbc = (conv_b[None, :SSD_INNER], conv_b[None, SSD_INNER:SSD_INNER + 512],
                     conv_b[None, SSD_INNER + 512:])
    a_pad = jnp.pad(a_log.reshape(SSD_GROUPS, 1, hg), ((0, 0), (0, 0), (0, LANES - hg)))
    a_col = a_log.reshape(SSD_GROUPS, hg, 1)
    d_lane = jnp.repeat(d_skip, SSD_HEAD_DIM)[None, :]
    gn = SSD_GROUPS * SSD_STATE
    npair = SSD_HEADS // 2
    full = lambda shape: pl.BlockSpec(shape, lambda bi, c: (0,) * len(shape))
    in_specs = [
        pl.BlockSpec((1, lb, SSD_INNER), lambda bi, c: (bi, c, 0)),
        pl.BlockSpec((1, lb, SSD_INNER), lambda bi, c: (bi, c, 1)),
        pl.BlockSpec((1, lb, gn), lambda bi, c: (bi, c, 8)),
        pl.BlockSpec((1, lb, gn), lambda bi, c: (bi, c, 9)),
        pl.BlockSpec((1, SSD_GROUPS, SSD_Q, LANES), lambda bi, c: (bi, 0, c, 0)),
        pl.BlockSpec((1, SSD_GROUPS, hg, SSD_Q), lambda bi, c: (bi, 0, 0, c)),
        full((SSD_GROUPS, 1, LANES)), full((SSD_GROUPS, hg, 1)),
        full((SSD_CONV, SSD_INNER)), full((SSD_CONV, gn)), full((SSD_CONV, gn)),
        full((1, SSD_INNER)), full((1, gn)), full((1, gn)),
        full((1, SSD_INNER)), full((1, SSD_INNER)),
    ]
    args = [pm, pm, pm, pm, dt, dtt, a_pad, a_col, cwx, cwb, cwc, cbx, cbb, cbc, d_lane, g_ssd[None, :]]
    if has_state:
        in_specs += [
            pl.BlockSpec((1, SSD_CONV - 1, SSD_INNER), lambda bi, c: (state_off + bi, 0, 0)),
            pl.BlockSpec((1, SSD_CONV - 1, gn), lambda bi, c: (state_off + bi, 0, 4)),
            pl.BlockSpec((1, SSD_CONV - 1, gn), lambda bi, c: (state_off + bi, 0, 5)),
            pl.BlockSpec((1, npair, LANES, LANES), lambda bi, c: (state_off + bi, 0, 0, 0)),
        ]
        args += [conv_prev, conv_prev, conv_prev, s0]
    y, s_out = pl.pallas_call(
        functools.partial(_ssd_kernel, lb=lb, has_state=has_state),
        grid=(b, nc),
        in_specs=in_specs,
        out_specs=[pl.BlockSpec((1, lb, SSD_INNER), lambda bi, c: (bi, c, 0)),
                   pl.BlockSpec((1, npair, LANES, LANES), lambda bi, c: (bi, 0, 0, 0))],
        out_shape=[jax.ShapeDtypeStruct((b, l, SSD_INNER), F32),
                   jax.ShapeDtypeStruct((b, npair, LANES, LANES), F32)],
        scratch_shapes=[pltpu.VMEM((npair, LANES, LANES), F32), pltpu.VMEM((8, SSD_INNER), F32),
                        pltpu.VMEM((8, gn), F32), pltpu.VMEM((8, gn), F32)],
        compiler_params=_cp(("parallel", "arbitrary")),
        name="ssd",
    )(*args)
    return y, s_out


def _dt_kernel(x_ref, g_ref, w_ref, wt_ref, bias_ref, biast_ref, o_ref, ot_ref):
    a = _rms(x_ref[...], g_ref[...]).astype(BF16)
    o_ref[...] = _softplus(jnp.dot(a, w_ref[...], preferred_element_type=F32) + bias_ref[...])
    t = lax.dot_general(wt_ref[...], a, (((1,), (1,)), ((), ())), preferred_element_type=F32)
    ot_ref[...] = _softplus(t + biast_ref[...])


def _dt_proj(x, gain, w_dt, dt_bias, *, tm=1024):
    m = x.shape[0]
    tm = min(tm, m)
    w_pad = jnp.pad(w_dt, ((0, 0), (0, LANES - SSD_HEADS))).astype(BF16)
    wt = w_dt.T.astype(BF16)
    bias_pad = jnp.pad(dt_bias, (0, LANES - SSD_HEADS))[None, :]
    return pl.pallas_call(
        _dt_kernel,
        grid=(m // tm,),
        in_specs=[pl.BlockSpec((tm, D_MODEL), lambda i: (i, 0)),
                  pl.BlockSpec((1, D_MODEL), lambda i: (0, 0)),
                  pl.BlockSpec((D_MODEL, LANES), lambda i: (0, 0)),
                  pl.BlockSpec((SSD_HEADS, D_MODEL), lambda i: (0, 0)),
                  pl.BlockSpec((1, LANES), lambda i: (0, 0)),
                  pl.BlockSpec((SSD_HEADS, 1), lambda i: (0, 0))],
        out_specs=[pl.BlockSpec((tm, LANES), lambda i: (i, 0)),
                   pl.BlockSpec((SSD_HEADS, tm), lambda i: (0, i))],
        out_shape=[jax.ShapeDtypeStruct((m, LANES), F32), jax.ShapeDtypeStruct((SSD_HEADS, m), F32)],
        compiler_params=_cp(("parallel",)),
        name="dt_proj",
    )(x, gain.reshape(1, -1), w_pad, wt, bias_pad, dt_bias[:, None])


def _gla_levels(n_valid):
    s, levels = 1, []
    while s < n_valid:
        levels.append(s)
        s *= 2
    return levels[::-1]


def _gla_kernel(*refs, lb, n_valid, has_state):
    if has_state:
        (v_ref, gg_ref, q_ref, k_ref, gf_ref, wf_ref, bf_ref, gn_ref, s0_ref, o_ref, so_ref, st_s) = refs
    else:
        (v_ref, gg_ref, q_ref, k_ref, gf_ref, wf_ref, bf_ref, gn_ref, o_ref, so_ref, st_s) = refs
    c = pl.program_id(1)
    qn = GLA_Q
    heads = range(GLA_HEADS)
    dk, dv = GLA_DK, GLA_DV

    @pl.when(c == 0)
    def _():
        for h in heads:
            if has_state:
                st_s[h] = s0_ref[0, h].T
            else:
                st_s[h] = jnp.zeros((dv, dk), F32)

    logits = _bdot(_pad_rows(gf_ref[0], qn), wf_ref[...]) + bf_ref[...]
    logf = -_softplus(-logits) * (1.0 / GLA_GATE_NORM)
    logf = jnp.where(_iota(logf.shape, 0) < n_valid, logf, 0.0)

    ri = _iota((qn, qn), 0)
    ci = _iota((qn, qn), 1)
    levels = _gla_levels(min(n_valid, qn))
    mats = [ri >= ci]
    masks = []
    for s in levels:
        same = _div(ri, s) == _div(ci, s)
        mats.append(same & (ri >= ci))
        mats.append(same)
        masks.append((_div(ri, 2 * s) == _div(ci, 2 * s)) & (_mod(_div(ri, s), 2) == 1)
                     & (_mod(_div(ci, s), 2) == 0))
    stack = jnp.concatenate([mm.astype(BF16) for mm in mats], axis=0)
    cums = _exact_left(stack, logf)

    qs = [_pad_rows(q_ref[0, :, h * dk:(h + 1) * dk], qn) * (dk ** -0.5) for h in heads]
    ks = [_pad_rows(k_ref[0, :, h * dk:(h + 1) * dk], qn) for h in heads]
    vs = [_pad_rows(v_ref[0, :, h * dv:(h + 1) * dv], qn) for h in heads]
    cbs = [cums[0:qn, h * dk:(h + 1) * dk] for h in heads]
    sts = [st_s[h] for h in heads]
    inter = [_bdot_nt(qs[h] * jnp.exp(cbs[h]), sts[h]) for h in heads]
    atts = [jnp.where(ri == ci, jnp.sum(qs[h] * ks[h], axis=-1, keepdims=True), 0.0) for h in heads]
    for idx in range(len(levels)):
        prods = []
        for h in heads:
            lq = cums[(1 + 2 * idx) * qn:(2 + 2 * idx) * qn, h * dk:(h + 1) * dk]
            bs = cums[(2 + 2 * idx) * qn:(3 + 2 * idx) * qn, h * dk:(h + 1) * dk]
            prods.append(_bdot_nt(qs[h] * jnp.exp(lq), ks[h] * jnp.exp(bs - lq)))
        atts = [atts[h] + jnp.where(masks[idx], prods[h], 0.0) for h in heads]
    os_ = [_bdot(atts[h], vs[h]) + inter[h] for h in heads]
    cbls = [cbs[h][qn - 1:qn, :] for h in heads]
    upds = [_bdot_tn(vs[h], ks[h] * jnp.exp(cbls[h] - cbs[h])) for h in heads]
    for h in heads:
        st_s[h] = sts[h] * jnp.exp(cbls[h]) + upds[h]
        o_ref[0, :, h * dv:(h + 1) * dv] = (_rms(os_[h][0:lb], gn_ref[...])
                                            * _silu(gg_ref[0, :, h * dv:(h + 1) * dv]))

    @pl.when(c == pl.num_programs(1) - 1)
    def _():
        for h in heads:
            so_ref[0, h] = st_s[h].T


def _gla(pg, gf, w_f2, b_f, g_gla, s0, *, lb, n_valid, state_off=0):
    b, l, _ = pg.shape
    nc = l // lb
    has_state = s0 is not None
    wf_pad = jnp.pad(w_f2, ((0, LANES - GLA_GATE_RANK), (0, 0))).astype(BF16)
    hv = GLA_HEADS * GLA_DV
    hk = GLA_HEADS * GLA_DK
    st_spec = pl.BlockSpec((1, GLA_HEADS, GLA_DK, GLA_DV), lambda bi, c: (bi, 0, 0, 0))
    in_specs = [
        pl.BlockSpec((1, lb, hv), lambda bi, c: (bi, c, 0)),
        pl.BlockSpec((1, lb, hv), lambda bi, c: (bi, c, 1)),
        pl.BlockSpec((1, lb, hk), lambda bi, c: (bi, c, 4)),
        pl.BlockSpec((1, lb, hk), lambda bi, c: (bi, c, 5)),
        pl.BlockSpec((1, lb, LANES), lambda bi, c: (bi, c, 0)),
        pl.BlockSpec((LANES, hk), lambda bi, c: (0, 0)),
        pl.BlockSpec((1, hk), lambda bi, c: (0, 0)),
        pl.BlockSpec((1, GLA_DV), lambda bi, c: (0, 0)),
    ]
    args = [pg, pg, pg, pg, gf, wf_pad, b_f[None, :], g_gla[None, :]]
    if has_state:
        in_specs.append(pl.BlockSpec((1, GLA_HEADS, GLA_DK, GLA_DV), lambda bi, c: (state_off + bi, 0, 0, 0)))
        args.append(s0)
    return pl.pallas_call(
        functools.partial(_gla_kernel, lb=lb, n_valid=n_valid, has_state=has_state),
        grid=(b, nc),
        in_specs=in_specs,
        out_specs=[pl.BlockSpec((1, lb, hv), lambda bi, c: (bi, c, 0)), st_spec],
        out_shape=[jax.ShapeDtypeStruct((b, l, hv), F32),
                   jax.ShapeDtypeStruct((b, GLA_HEADS, GLA_DK, GLA_DV), F32)],
        scratch_shapes=[pltpu.VMEM((GLA_HEADS, GLA_DV, GLA_DK), F32)],
        compiler_params=_cp(("parallel", "arbitrary")),
        name="gla",
    )(*args)


def _rwkv_prep_kernel(*refs, tiles_per_seq, prev_given):
    if prev_given:
        (cur_ref, prev_ref, mu_ref, w12_ref, wg2_ref, w0_ref, a0_ref, kk_ref, ka_ref,
         r_o, lw_o, k_o, v_o, kk_o, a_o, g_o) = refs
    else:
        (cur_ref, mu_ref, w12_ref, wg2_ref, w0_ref, a0_ref, kk_ref, ka_ref,
         r_o, lw_o, k_o, v_o, kk_o, a_o, g_o, last_s) = refs
    d = D_MODEL
    cur = cur_ref[...]
    if prev_given:
        prev = prev_ref[...]
    else:
        i = pl.program_id(0)

        @pl.when(i % tiles_per_seq == 0)
        def _():
            last_s[...] = jnp.zeros_like(last_s)

        row = _iota(cur.shape, 0)
        first = jnp.broadcast_to(last_s[7:8, :], cur.shape)
        prev = jnp.where(row == 0, first, pltpu.roll(cur, 1, 0))
        last_s[...] = cur[cur.shape[0] - 8:]
    mixed = cur + (prev - cur) * mu_ref[...]
    r = mixed[:, 0:d]
    kc = mixed[:, d:2 * d]
    vc = mixed[:, 2 * d:3 * d]
    t1 = mixed[:, 3 * d:3 * d + LANES]
    lane = _iota(t1.shape, 1)
    t1 = jnp.where(lane < 64, jnp.tanh(t1), t1)
    lora = _bdot(t1, w12_ref[...])
    logw = -_softplus(-(w0_ref[...] + lora[:, 0:d])) - 0.5
    a = _sigmoid(a0_ref[...] + lora[:, d:2 * d])
    g = _bdot(_sigmoid(mixed[:, 3 * d + LANES:3 * d + 2 * LANES]), wg2_ref[...])
    kk = kc * kk_ref[...]
    ss = _seg_sum(kk * kk, _seg_ones(RWKV_HEAD_DIM))
    kk = kk * lax.rsqrt(jnp.maximum(ss, 1e-24))
    r_o[...] = r
    lw_o[...] = -jnp.exp(logw)
    k_o[...] = kc * (1.0 + (a - 1.0) * ka_ref[...])
    v_o[...] = vc
    kk_o[...] = kk
    a_o[...] = a
    g_o[...] = g


def _rwkv_prep(cur, prev, mu, w12, wg2, w0, a0, k_k, k_a, *, tm, seq_len):
    t = cur.shape[0]
    tm = min(tm, t)
    prev_given = prev is not None
    row_spec = pl.BlockSpec((tm, RWKV_SHIFT_DIM), lambda i: (i, 0))
    vec = lambda n: pl.BlockSpec((1, n), lambda i: (0, 0))
    in_specs = [row_spec] + ([row_spec] if prev_given else []) + [
        vec(RWKV_SHIFT_DIM),
        pl.BlockSpec((LANES, 2 * D_MODEL), lambda i: (0, 0)),
        pl.BlockSpec((LANES, D_MODEL), lambda i: (0, 0)),
        vec(D_MODEL), vec(D_MODEL), vec(D_MODEL), vec(D_MODEL)]
    args = [cur] + ([prev] if prev_given else []) + [mu[None, :], w12, wg2, w0[None, :], a0[None, :],
                                                    k_k[None, :], k_a[None, :]]
    out_spec = pl.BlockSpec((tm, D_MODEL), lambda i: (i, 0))
    return pl.pallas_call(
        functools.partial(_rwkv_prep_kernel, tiles_per_seq=max(seq_len // tm, 1), prev_given=prev_given),
        grid=(t // tm,),
        in_specs=in_specs,
        out_specs=[out_spec] * 7,
        out_shape=[jax.ShapeDtypeStruct((t, D_MODEL), F32)] * 7,
        scratch_shapes=[] if prev_given else [pltpu.VMEM((8, RWKV_SHIFT_DIM), F32)],
        compiler_params=_cp(("arbitrary",)),
        name="rwkv_prep",
    )(*args)


def _rwkv_kernel(*refs, lb, has_state):
    if has_state:
        (r_ref, lw_ref, k_ref, v_ref, kk_ref, a_ref, g_ref, rk_ref, gw_ref, gb_ref, s0_ref,
         y_ref, so_ref, s_s) = refs
    else:
        (r_ref, lw_ref, k_ref, v_ref, kk_ref, a_ref, g_ref, rk_ref, gw_ref, gb_ref,
         y_ref, so_ref, s_s) = refs
    c = pl.program_id(1)
    qn = RWKV_Q
    hd = RWKV_HEAD_DIM

    @pl.when(c == 0)
    def _():
        if has_state:
            s_s[...] = s0_ref[0]
        else:
            s_s[...] = jnp.zeros_like(s_s)

    ri = _iota((qn, qn), 0)
    ci = _iota((qn, qn), 1)
    lw_all = _pad_rows(lw_ref[0], qn)
    cw_all = _exact_left(ri >= ci, lw_all)
    lane = _iota((qn, LANES), 1)
    low = lane < hd
    rowq = _iota((qn, LANES), 0)
    lane_m = _mod(lane, hd)
    strict = rowq > lane_m
    incl = rowq >= lane_m
    r2 = _iota((2 * qn, LANES), 0)
    l2 = _iota((2 * qn, LANES), 1)
    bd = (r2 < qn) == (l2 < hd)
    strict_bd = bd & (_mod(r2, qn) > _mod(l2, hd))
    eye = (r2 == l2).astype(F32)
    steps = max(int(math.ceil(math.log2(min(lb, qn)))), 2)
    pairs = range(RWKV_HEADS // 2)
    sls = [slice(p * LANES, (p + 1) * LANES) for p in pairs]
    split = lambda x: jnp.concatenate([jnp.where(low, x, 0.0), jnp.where(low, 0.0, x)], axis=0)
    rs, ks, vs, bbs, ats, rts, cwls, eqs, lsts, rsts = ([] for _ in range(10))
    for sl in sls:
        lw = lw_all[:, sl]
        cw = cw_all[:, sl]
        r = _pad_rows(r_ref[0, :, sl], qn)
        k = _pad_rows(k_ref[0, :, sl], qn)
        v = _pad_rows(v_ref[0, :, sl], qn)
        kk = _pad_rows(kk_ref[0, :, sl], qn)
        av = _pad_rows(a_ref[0, :, sl], qn)
        cwl = cw[qn - 1:qn, :]
        e_neg = jnp.exp(-cw)
        bb = kk * av
        at = -kk * jnp.exp(cw - lw)
        rt = r * jnp.exp(cw)
        rs.append(r)
        ks.append(k)
        vs.append(v)
        bbs.append(bb)
        ats.append(at)
        rts.append(rt)
        cwls.append(cwl)
        eqs.append(jnp.exp(cwl - cw))
        lsts.append(jnp.concatenate([split(at), split(rt)], axis=0))
        rsts.append(jnp.concatenate([bb * e_neg, k * e_neg], axis=0))
    pms = [_bdot_nt(lsts[p], rsts[p]) for p in pairs]
    s_bds = [s_s[p] for p in pairs]
    ars = [_bdot_nt(jnp.concatenate([ats[p], rts[p]], axis=0), s_bds[p]) for p in pairs]
    n_bds, aaks, arks = [], [], []
    for pm in pms:
        p0, p1, p2, p3 = pm[0:qn], pm[qn:2 * qn], pm[2 * qn:3 * qn], pm[3 * qn:4 * qn]
        n_bds.append(jnp.where(strict_bd, jnp.concatenate([p0, pltpu.roll(p1, hd, 1)], axis=0), 0.0))
        aaks.append(jnp.where(strict, jnp.where(low, pltpu.roll(p0, hd, 1), p1), 0.0))
        arb = jnp.where(incl, jnp.where(low, p2, pltpu.roll(p3, hd, 1)), 0.0)
        ark = jnp.where(incl, jnp.where(low, pltpu.roll(p2, hd, 1), p3), 0.0)
        arks.append(jnp.concatenate([arb, ark], axis=1))
    mpows = [_bdot(n, n) for n in n_bds]
    t_invs = [eye + n for n in n_bds]
    for _ in range(steps - 2):
        prods = [_bdot(jnp.concatenate([t_invs[p], mpows[p]], axis=0), mpows[p]) for p in pairs]
        t_invs = [t_invs[p] + prods[p][0:2 * qn] for p in pairs]
        mpows = [prods[p][2 * qn:4 * qn] for p in pairs]
    t_invs = [t_invs[p] + _bdot(t_invs[p], mpows[p]) for p in pairs]
    v_ms = [split(v) for v in vs]
    rhss = [ars[p][0:qn] + _bdot(aaks[p], v_ms[p]) for p in pairs]
    u2s = [_bdot(t_invs[p], split(rhss[p])) for p in pairs]
    us = [u2[0:qn] + u2[qn:2 * qn] for u2 in u2s]
    ys = [ars[p][qn:2 * qn] + _bdot(arks[p], jnp.concatenate([split(us[p]), v_ms[p]], axis=0)) for p in pairs]
    upds = [_bdot_tn(jnp.concatenate([us[p], vs[p]], axis=0),
                     jnp.concatenate([bbs[p] * eqs[p], ks[p] * eqs[p]], axis=0)) for p in pairs]
    for p in pairs:
        s_s[p] = s_bds[p] * jnp.exp(cwls[p]) + jnp.where(bd, upds[p], 0.0)
    inv_n = 1.0 / hd
    lowb = low[0:lb]

    def half_sum(x):
        lo = jnp.sum(jnp.where(lowb, x, 0.0), axis=-1, keepdims=True)
        hi = jnp.sum(jnp.where(lowb, 0.0, x), axis=-1, keepdims=True)
        return jnp.where(lowb, lo, hi)

    for p, sl in enumerate(sls):
        y = ys[p][0:lb]
        mu = half_sum(y) * inv_n
        yc = y - mu
        var = half_sum(yc * yc) * inv_n
        yn = yc * lax.rsqrt(var + RWKV_GN_EPS) * gw_ref[:, sl] + gb_ref[:, sl]
        bonus = half_sum(rs[p][0:lb] * ks[p][0:lb] * rk_ref[:, sl]) * vs[p][0:lb]
        y_ref[0, :, sl] = (yn + bonus) * g_ref[0, :, sl]

    @pl.when(c == pl.num_programs(1) - 1)
    def _():
        so_ref[0] = s_s[...]


def _rwkv(r, lw, k, v, kk, a, g, r_k, gn_w, gn_b, s0_bd, *, lb):
    b, l, _ = r.shape
    nc = l // lb
    has_state = s0_bd is not None
    tok = pl.BlockSpec((1, lb, D_MODEL), lambda bi, c: (bi, c, 0))
    vec = pl.BlockSpec((1, D_MODEL), lambda bi, c: (0, 0))
    st = pl.BlockSpec((1, RWKV_HEADS // 2, LANES, LANES), lambda bi, c: (bi, 0, 0, 0))
    in_specs = [tok] * 7 + [vec] * 3 + ([st] if has_state else [])
    args = [r, lw, k, v, kk, a, g, r_k.reshape(1, -1), gn_w[None, :], gn_b[None, :]] + ([s0_bd] if has_state else [])
    return pl.pallas_call(
        functools.partial(_rwkv_kernel, lb=lb, has_state=has_state),
        grid=(b, nc),
        in_specs=in_specs,
        out_specs=[tok, st],
        out_shape=[jax.ShapeDtypeStruct((b, l, D_MODEL), F32),
                   jax.ShapeDtypeStruct((b, RWKV_HEADS // 2, LANES, LANES), F32)],
        scratch_shapes=[pltpu.VMEM((RWKV_HEADS // 2, LANES, LANES), F32)],
        compiler_params=_cp(("parallel", "arbitrary")),
        name="rwkv",
    )(*args)


def _memattn_kernel(q_ref, mk_ref, mv_ref, o_ref):
    scale = MEM_HEAD_DIM ** -0.5
    sls = [slice(h * MEM_HEAD_DIM, (h + 1) * MEM_HEAD_DIM) for h in range(MEM_HEADS)]
    ss = [_bdot_nt(q_ref[0, :, sl], mk_ref[0, 0, :, sl]) * scale for sl in sls]
    ms = [jnp.max(s, axis=-1, keepdims=True) for s in ss]
    ps = [jnp.exp(s - m) for s, m in zip(ss, ms)]
    ls = [jnp.sum(p, axis=-1, keepdims=True) for p in ps]
    outs = [_bdot(p / l, mv_ref[0, 0, :, sl]) for p, l, sl in zip(ps, ls, sls)]
    for sl, o in zip(sls, outs):
        o_ref[0, :, sl] = o


def _mem_attn(q, mk, mv, layer, *, tq):
    b, l, _ = q.shape
    tq = min(tq, l)
    mem_spec = pl.BlockSpec((1, 1, N_MEM, D_MODEL), lambda bi, i: (layer, bi, 0, 0))
    return pl.pallas_call(
        _memattn_kernel,
        grid=(b, l // tq),
        in_specs=[pl.BlockSpec((1, tq, D_MODEL), lambda bi, i: (bi, i, 0)), mem_spec, mem_spec],
        out_specs=pl.BlockSpec((1, tq, D_MODEL), lambda bi, i: (bi, i, 0)),
        out_shape=jax.ShapeDtypeStruct((b, l, D_MODEL), F32),
        compiler_params=_cp(("parallel", "arbitrary")),
        name="mem_attn",
    )(q, mk, mv)


def _pad_tokens(x, lp):
    b, l = x.shape[0], x.shape[1]
    if l == lp:
        return x
    return jnp.pad(x, ((0, 0), (0, lp - l)) + ((0, 0),) * (x.ndim - 2))


def _even_layer(x, w, e, grp):
    b, l = grp["b"], grp["l"]
    sample = grp["sample"]
    gain = w["norm_mix"][2 * e]
    pm = _mm([x], [w["e_w_main"][e]], gain=gain, name="e_in_main")
    v = _mm([x], [w["e_w_v"][e]], gain=gain, name="e_in_v")
    dt, dtt = _dt_proj(x, gain, w["e_w_dt"][e], w["e_dt_bias"][e])
    hg = SSD_HEADS // SSD_GROUPS
    lp = l if not sample else 8
    lq = max(lp, SSD_Q)
    dt4 = dt[:, :SSD_HEADS].reshape(b, l, SSD_GROUPS, hg).transpose(0, 2, 1, 3)
    dt4 = jnp.pad(dt4, ((0, 0), (0, 0), (0, lq - l), (0, LANES - hg)))
    dtt4 = dtt.reshape(SSD_GROUPS, hg, b, l).transpose(2, 0, 1, 3)
    dtt4 = jnp.pad(dtt4, ((0, 0), (0, 0), (0, 0), (0, lq - l)))
    pm_seq = pm.reshape(b, l, -1)
    pm3 = _pad_tokens(pm_seq, lp)
    kc1 = SSD_CONV - 1
    xbc_tail = pm_seq[:, max(l - kc1, 0):, SSD_INNER:SSD_INNER + XBC_DIM]
    if sample:
        conv_prev = grp["conv"].reshape(N_EVEN * b, kc1, XBC_DIM)
        s0 = grp["ssd"].reshape(N_EVEN * b, SSD_HEADS // 2, LANES, LANES)
        conv_new = jnp.concatenate([grp["conv"][e], xbc_tail], axis=1)[:, -kc1:]
    else:
        conv_prev, s0 = None, None
        assert l >= kc1
        conv_new = xbc_tail
    y_ssd, s_new = _ssd(pm3, dt4, dtt4, w["e_conv_w"][e], w["e_conv_b"][e], w["e_a_log"][e], w["e_d_skip"][e],
                        w["e_g_ssd"][e], conv_prev, s0, lb=min(lp, SSD_Q), state_off=e * b)
    y_ssd = y_ssd[:, :l].reshape(b * l, SSD_INNER)
    s_new = s_new.reshape(b, SSD_HEADS, SSD_HEAD_DIM, SSD_STATE)
    q0 = grp["q0"]
    cos_t, sin_t = _rope_tables(q0 + jnp.arange(l))
    if sample:
        cos_t, sin_t = jnp.tile(cos_t, (b, 1)), jnp.tile(sin_t, (b, 1))
        q_r, k_r = _qk_prep(pm, 5, 6, cos_t, sin_t, w["e_g_q"][e], w["e_g_k"][e], tq=b * l, with_kmean=False)
        th = lambda t: t.reshape(b, l * MOBA_HEADS, MOBA_HEAD_DIM)
        o = _moba_sample(grp["page_table"], th(q_r), th(k_r), th(v), grp["cache_k"], grp["cache_v"],
                         layer_off=e * grp["n_pool"]).reshape(b * l, D_MODEL)
    else:
        q_r, k_r, kmean = _qk_prep(pm, 5, 6, cos_t, sin_t, w["e_g_q"][e], w["e_g_k"][e], tq=MOBA_BLOCK,
                                   with_kmean=True)
        o = _moba_prompt(q_r, k_r, v, kmean.reshape(b, l // MOBA_BLOCK, D_MODEL), b=b, l=l)
    x = _mm([y_ssd, o], [w["e_w_out_a"][e], w["e_w_out_b"][e]], res=x, tm=512, name="e_out")
    hd = (b, l, MOBA_HEADS, MOBA_HEAD_DIM)
    return x, s_new, conv_new, k_r.reshape(hd), v.reshape(hd)


def _odd_layer(x, w, o, grp):
    b, l = grp["b"], grp["l"]
    sample = grp["sample"]
    gain = w["norm_mix"][2 * o + 1]
    cur = _mm([x], [w["o_w_cur"][o]], gain=gain, tn=1664, name="o_in_cur")
    pg = _mm([x], [w["o_w_g"][o]], gain=gain, name="o_in_g")
    gf = _mm([x], [w["o_w_gf"][o]], gain=gain, name="o_in_gf")
    cur3 = cur.reshape(b, l, RWKV_SHIFT_DIM)
    if sample:
        prev = jnp.concatenate([grp["shift"][o][:, None], cur3[:, :-1]], axis=1).reshape(b * l, RWKV_SHIFT_DIM)
    else:
        prev = None
    vecs = _rwkv_prep(cur, prev, w["o_mu"][o], w["o_w12"][o], w["o_w_g2"][o], w["o_w0"][o], w["o_a0"][o],
                      w["o_k_k"][o], w["o_k_a"][o], tm=512, seq_len=l)
    lp = l if not sample else 8
    vecs3 = [_pad_tokens(t.reshape(b, l, D_MODEL), lp) for t in vecs]
    if sample:
        s0 = grp["rwkv"][o].reshape(b, RWKV_HEADS // 2, 2, RWKV_HEAD_DIM, RWKV_HEAD_DIM)
        z = jnp.zeros_like(s0[:, :, 0])
        s0_bd = jnp.concatenate([jnp.concatenate([s0[:, :, 0], z], axis=-1),
                                 jnp.concatenate([z, s0[:, :, 1]], axis=-1)], axis=-2)
    else:
        s0_bd = None
    y_r, s_bd = _rwkv(*vecs3, w["o_r_k"][o], w["o_gn_w"][o], w["o_gn_b"][o], s0_bd, lb=min(lp, RWKV_Q))
    y_r = y_r[:, :l].reshape(b * l, D_MODEL)
    hd = RWKV_HEAD_DIM
    s_r = jnp.stack([s_bd[:, :, :hd, :hd], s_bd[:, :, hd:, hd:]], axis=2).reshape(b, RWKV_HEADS, hd, hd)
    pg3 = _pad_tokens(pg.reshape(b, l, -1), lp)
    gf3 = _pad_tokens(gf.reshape(b, l, LANES), lp)
    s0_g = grp["gla"].reshape(N_ODD * b, GLA_HEADS, GLA_DK, GLA_DV) if sample else None
    og, s_g = _gla(pg3, gf3, w["o_w_f2"][o], w["o_b_f"][o], w["o_g_gla"][o], s0_g,
                   lb=min(lp, GLA_Q), n_valid=min(l, GLA_Q), state_off=o * b)
    og = og[:, :l].reshape(b * l, GLA_HEADS * GLA_DV)
    x = _mm([y_r, og], [w["o_w_out_a"][o], w["o_w_out_b"][o]], res=x, tm=512, name="o_out")
    return x, s_r, cur3[:, -1], s_g


def _trunk(x, w, grp):
    b, l = grp["b"], grp["l"]
    outs = {k: [] for k in ("ssd", "conv", "k", "v", "rwkv", "shift", "gla")}
    for layer in range(DEPTH):
        if layer % 2 == 0:
            x, s, cnew, k, v = _even_layer(x, w, layer // 2, grp)
            outs["ssd"].append(s)
            outs["conv"].append(cnew)
            outs["k"].append(k)
            outs["v"].append(v)
        else:
            x, s_r, sh, s_g = _odd_layer(x, w, layer // 2, grp)
            outs["rwkv"].append(s_r)
            outs["shift"].append(sh)
            outs["gla"].append(s_g)
        qm = _mm([x], [w["m_w_q"][layer]], gain=w["norm_mem"][layer], head_gain=w["m_g_q"][layer], name="mem_q")
        lp = 8 if grp["sample"] else l
        om = _mem_attn(_pad_tokens(qm.reshape(b, l, D_MODEL), lp), grp["mem_k"], grp["mem_v"], layer, tq=512)
        x = _mm([om[:, :l].reshape(b * l, D_MODEL)], [w["m_w_o"][layer]], res=x, name="mem_o")
        x = _ffn(x, w["norm_ffn"][layer], w["f_w_g"][layer], w["f_w_u"][layer], w["f_w_d"][layer])
    return x, {k: jnp.stack(v) for k, v in outs.items()}


def kernel(x_prompt, x_sample, cache_moba_k, cache_moba_v, state_ssd, state_ssd_conv, state_rwkv, state_rwkv_shift, state_gla, cache_mem_k, cache_mem_v, page_table, mem_prompt, norm_mix, norm_mem, norm_memtok, norm_ffn, e_w_in, e_conv_w, e_conv_b, e_dt_bias, e_a_log, e_d_skip, e_g_ssd, e_g_q, e_g_k, e_w_out, o_w_in, o_mu, o_w0, o_w_w2, o_a0, o_w_a2, o_w_g2, o_k_k, o_k_a, o_r_k, o_gn_w, o_gn_b, o_w_f2, o_b_f, o_g_gla, o_w_out, m_w_q, m_w_kv, m_g_q, m_g_k, m_w_o, f_w_gu, f_w_down):
    bp, lp, _ = x_prompt.shape
    bs, ls, _ = x_sample.shape
    n_pool = cache_moba_k.shape[1]
    n_pages = page_table.shape[1]
    c_dt = SSD_INNER + XBC_DIM
    c_q = c_dt + SSD_HEADS
    c_cur = RWKV_SHIFT_DIM
    z64 = jnp.zeros((N_ODD, 64, D_MODEL), F32)
    w = dict(
        norm_mix=norm_mix, norm_mem=norm_mem, norm_ffn=norm_ffn,
        e_w_main=jnp.concatenate([e_w_in[:, :, :c_dt], e_w_in[:, :, c_q:c_q + 2 * D_MODEL]], axis=-1).astype(BF16),
        e_w_v=e_w_in[:, :, c_q + 2 * D_MODEL:].astype(BF16),
        e_w_dt=e_w_in[:, :, c_dt:c_q],
        e_conv_w=e_conv_w, e_conv_b=e_conv_b, e_dt_bias=e_dt_bias, e_a_log=e_a_log, e_d_skip=e_d_skip,
        e_g_ssd=e_g_ssd, e_g_q=e_g_q, e_g_k=e_g_k,
        e_w_out_a=e_w_out[:, :SSD_INNER].astype(BF16), e_w_out_b=e_w_out[:, SSD_INNER:].astype(BF16),
        o_w_cur=o_w_in[:, :, :c_cur].astype(BF16),
        o_w_g=jnp.concatenate([o_w_in[:, :, c_cur + 2 * GLA_KEY_DIM:c_cur + 2 * GLA_KEY_DIM + 2 * D_MODEL],
                               o_w_in[:, :, c_cur:c_cur + 2 * GLA_KEY_DIM]], axis=-1).astype(BF16),
        o_w_gf=jnp.pad(o_w_in[:, :, c_cur + 2 * GLA_KEY_DIM + 2 * D_MODEL:],
                       ((0, 0), (0, 0), (0, LANES - GLA_GATE_RANK))).astype(BF16),
        o_mu=o_mu, o_w0=o_w0, o_a0=o_a0, o_k_k=o_k_k, o_k_a=o_k_a, o_r_k=o_r_k, o_gn_w=o_gn_w, o_gn_b=o_gn_b,
        o_w12=jnp.concatenate([jnp.concatenate([o_w_w2, z64], axis=-1),
                               jnp.concatenate([z64, o_w_a2], axis=-1)], axis=1).astype(BF16),
        o_w_g2=o_w_g2.astype(BF16), o_w_f2=o_w_f2, o_b_f=o_b_f, o_g_gla=o_g_gla,
        o_w_out_a=o_w_out[:, :D_MODEL].astype(BF16), o_w_out_b=o_w_out[:, D_MODEL:].astype(BF16),
        m_w_q=m_w_q.astype(BF16), m_g_q=m_g_q, m_w_o=m_w_o.astype(BF16),
        f_w_g=f_w_gu[:, :, :D_FF].astype(BF16), f_w_u=f_w_gu[:, :, D_FF:].astype(BF16),
        f_w_d=f_w_down.astype(BF16),
    )
    mem2 = mem_prompt.reshape(bp * N_MEM, D_MODEL)
    mks, mvs = [], []
    for layer in range(DEPTH):
        wkv = m_w_kv[layer].astype(BF16)
        mks.append(_mm([mem2], [wkv[:, :D_MODEL]], gain=norm_memtok[layer], head_gain=m_g_k[layer], name="mem_k"))
        mvs.append(_mm([mem2], [wkv[:, D_MODEL:]], gain=norm_memtok[layer], name="mem_v"))
    mem_k_p = jnp.stack(mks).reshape(DEPTH, bp, N_MEM, D_MODEL)
    mem_v_p = jnp.stack(mvs).reshape(DEPTH, bp, N_MEM, D_MODEL)

    grp_p = dict(b=bp, l=lp, sample=False, q0=0, mem_k=mem_k_p, mem_v=mem_v_p)
    y_p, o_p = _trunk(x_prompt.reshape(bp * lp, D_MODEL), w, grp_p)

    grp_s = dict(b=bs, l=ls, sample=True, q0=n_pages * PAGE_SIZE, page_table=page_table, n_pool=n_pool,
                 cache_k=cache_moba_k.reshape(N_EVEN * n_pool, PAGE_SIZE * MOBA_HEADS, MOBA_HEAD_DIM),
                 cache_v=cache_moba_v.reshape(N_EVEN * n_pool, PAGE_SIZE * MOBA_HEADS, MOBA_HEAD_DIM),
                 ssd=state_ssd, conv=state_ssd_conv, rwkv=state_rwkv, shift=state_rwkv_shift, gla=state_gla,
                 mem_k=cache_mem_k.astype(BF16).reshape(DEPTH, bs, N_MEM, D_MODEL),
                 mem_v=cache_mem_v.astype(BF16).reshape(DEPTH, bs, N_MEM, D_MODEL))
    y_s, o_s = _trunk(x_sample.reshape(bs * ls, D_MODEL), w, grp_s)

    mem_shape = (DEPTH, bp, N_MEM, MEM_HEADS, MEM_HEAD_DIM)
    return (y_p.reshape(bp, lp, D_MODEL), y_s.reshape(bs, ls, D_MODEL),
            o_p["k"], o_p["v"], o_s["k"], o_s["v"], o_p["ssd"], o_s["ssd"], o_p["conv"], o_s["conv"],
            o_p["rwkv"], o_s["rwkv"], o_p["shift"], o_s["shift"], o_p["gla"], o_s["gla"],
            mem_k_p.reshape(mem_shape), mem_v_p.reshape(mem_shape))
```
